```python
import math, functools
import jax, jax.numpy as jnp
from jax import lax
import numpy as np

D_MODEL = 1024
BATCH = 4
SEQ = 4096
DEPTH = 2
DEC_BATCH = 32
DEC_SEQ = 8
PAST_LEN = 16384
PAGE_SIZE = 128

D_SSM = D_MODEL // 2
SSM_GROUP_CH = 16
SSM_GROUPS = D_SSM // SSM_GROUP_CH
SSM_STATE = 64
N_HEADS = 8
HEAD_DIM = 64
KV_HEADS = 2
D_ATT = N_HEADS * HEAD_DIM
ROT_DIM = HEAD_DIM // 4
ROPE_THETA = 500000.0
IDX_HEADS = 8
IDX_DIM = 64
TOPK_MAX = 256
Q_BLOCK = 128
D_FF = -(-8 * D_MODEL // (3 * 256)) * 256
EPS = 1e-6

IN_SIZES = (D_SSM, D_ATT, KV_HEADS * HEAD_DIM, KV_HEADS * HEAD_DIM,
            IDX_HEADS * IDX_DIM, IDX_DIM, IDX_HEADS, D_MODEL, D_MODEL)
IN_SPLITS = tuple(int(s) for s in np.cumsum(IN_SIZES)[:-1])
D_IN = int(sum(IN_SIZES))

kernel_name = 'hybrid_s5_dsa_decoder_step'


def rms_norm(x, g):
    x32 = x.astype(jnp.float32)
    y = x32 * lax.rsqrt(jnp.mean(x32 * x32, axis=-1, keepdims=True) + EPS)
    return (y * g.astype(jnp.float32)).astype(x.dtype)


def rope_partial(x, pos):
    half = ROT_DIM // 2
    freqs = ROPE_THETA ** (-jnp.arange(half, dtype=jnp.float32) / half)
    ang = pos.astype(jnp.float32)[:, None] * freqs[None, :]
    cos = jnp.cos(ang)[:, None, :]
    sin = jnp.sin(ang)[:, None, :]
    xr = x[..., :ROT_DIM].astype(jnp.float32)
    x1, x2 = xr[..., :half], xr[..., half:]
    rot = jnp.concatenate([x1 * cos - x2 * sin, x1 * sin + x2 * cos], axis=-1).astype(x.dtype)
    return jnp.concatenate([rot, x[..., ROT_DIM:]], axis=-1)


def s5_branch(u, h0_re, h0_im, p):
    n_b, n_t, _ = u.shape
    f32 = jnp.float32
    u32 = u.astype(f32).reshape(n_b, n_t, SSM_GROUPS, SSM_GROUP_CH)
    a_re = p['a_re'].astype(f32)
    a_im = p['a_im'].astype(f32)
    dt = jnp.exp(p['log_dt'].astype(f32))[:, None]
    mag = jnp.exp(dt * a_re)
    abar_re = mag * jnp.cos(dt * a_im)
    abar_im = mag * jnp.sin(dt * a_im)
    nr, ni = abar_re - 1.0, abar_im
    den = a_re * a_re + a_im * a_im
    coef_re = (nr * a_re + ni * a_im) / den
    coef_im = (ni * a_re - nr * a_im) / den
    bu_re = jnp.einsum('btgc,gpc->btgp', u32, p['b_re'].astype(f32))
    bu_im = jnp.einsum('btgc,gpc->btgp', u32, p['b_im'].astype(f32))
    bb_re = coef_re * bu_re - coef_im * bu_im
    bb_im = coef_re * bu_im + coef_im * bu_re
    h0r, h0i = h0_re.astype(f32), h0_im.astype(f32)
    bb_re = bb_re.at[:, 0].add(abar_re * h0r - abar_im * h0i)
    bb_im = bb_im.at[:, 0].add(abar_re * h0i + abar_im * h0r)
    ar_t = jnp.broadcast_to(abar_re, bb_re.shape)
    ai_t = jnp.broadcast_to(abar_im, bb_im.shape)

    def combine(e1, e2):
        a1r, a1i, b1r, b1i = e1
        a2r, a2i, b2r, b2i = e2
        return (a2r * a1r - a2i * a1i, a2r * a1i + a2i * a1r,
                a2r * b1r - a2i * b1i + b2r, a2r * b1i + a2i * b1r + b2i)

    _, _, h_re, h_im = lax.associative_scan(combine, (ar_t, ai_t, bb_re, bb_im), axis=1)
    y = (jnp.einsum('btgp,gcp->btgc', h_re, p['c_re'].astype(f32))
         - jnp.einsum('btgp,gcp->btgc', h_im, p['c_im'].astype(f32))
         + p['d'].astype(f32).reshape(SSM_GROUPS, SSM_GROUP_CH) * u32)
    g = jax.nn.gelu(y.reshape(n_b, n_t, D_SSM))
    out = g * jax.nn.sigmoid(g @ p['w_glu'].astype(f32))
    return out.astype(u.dtype), h_re[:, -1].astype(h0_re.dtype), h_im[:, -1].astype(h0_im.dtype)


def select_keys(qi, wi, ki, qpos, topk):
    dots = jnp.einsum('bthd,bsd->bths', qi, ki, preferred_element_type=jnp.float32)
    score = jnp.einsum('bth,bths->bts', wi.astype(jnp.float32), jax.nn.relu(dots))
    admissible = jnp.arange(ki.shape[1])[None, None, :] <= qpos[None, :, None]
    score = jnp.where(admissible, score, -jnp.inf)
    return lax.top_k(score, topk)[1]


def sparse_softmax_attend(q, ks, vs, valid):
    n_b, n_t = q.shape[:2]
    qg = q.reshape(n_b, n_t, KV_HEADS, N_HEADS // KV_HEADS, HEAD_DIM)
    s = jnp.einsum('btgrd,btkgd->btgrk', qg, ks, preferred_element_type=jnp.float32) * (HEAD_DIM ** -0.5)
    s = jnp.where(valid[:, :, None, None, :], s, -jnp.inf)
    pr = jax.nn.softmax(s, axis=-1).astype(vs.dtype)
    o = jnp.einsum('btgrk,btkgd->btgrd', pr, vs)
    return o.reshape(n_b, n_t, D_ATT)


gather_rows = jax.vmap(lambda a, i: a[i])


def prompt_attention(q, k, v, qi, ki, wi):
    n_b, n_s = q.shape[:2]
    topk = min(TOPK_MAX, n_s // 4)

    def block(i):
        start = i * Q_BLOCK
        qb = lax.dynamic_slice_in_dim(q, start, Q_BLOCK, axis=1)
        qib = lax.dynamic_slice_in_dim(qi, start, Q_BLOCK, axis=1)
        wib = lax.dynamic_slice_in_dim(wi, start, Q_BLOCK, axis=1)
        qpos = start + jnp.arange(Q_BLOCK)
        idx = select_keys(qib, wib, ki, qpos, topk)
        return sparse_softmax_attend(qb, gather_rows(k, idx), gather_rows(v, idx),
                                     idx <= qpos[None, :, None])

    out = lax.map(block, jnp.arange(n_s // Q_BLOCK))
    return out.transpose(1, 0, 2, 3).reshape(n_b, n_s, D_ATT)


def sample_attention(cache_k, cache_v, cache_kidx, page_table, layer, q, k, v, qi, ki, wi):
    n_b, n_new = q.shape[:2]
    past = page_table.shape[1] * PAGE_SIZE
    topk = min(TOPK_MAX, (past + n_new) // 4)
    kidx_past = cache_kidx[layer, page_table].reshape(n_b, past, IDX_DIM)
    kidx_all = jnp.concatenate([kidx_past.astype(ki.dtype), ki], axis=1)
    qpos = past + jnp.arange(n_new)
    idx = select_keys(qi, wi, kidx_all, qpos, topk)
    in_past = (idx < past)[..., None, None]
    pidx = jnp.minimum(idx, past - 1)
    phys = gather_rows(page_table, pidx // PAGE_SIZE)
    off = pidx % PAGE_SIZE
    nidx = jnp.clip(idx - past, 0, n_new - 1)
    ks = jnp.where(in_past, cache_k[layer, phys, off].astype(k.dtype), gather_rows(k, nidx))
    vs = jnp.where(in_past, cache_v[layer, phys, off].astype(v.dtype), gather_rows(v, nidx))
    return sparse_softmax_attend(q, ks, vs, idx <= qpos[None, :, None])


def trunk_layer(x, pos, h0_re, h0_im, attend_fn, p):
    n_b, n_t, _ = x.shape
    h = rms_norm(x, p['g_mix'])
    z = h @ p['w_in']
    u, q, k, v, qi, ki, wi, ga, gb = jnp.split(z, IN_SPLITS, axis=-1)
    y_a, hT_re, hT_im = s5_branch(u, h0_re, h0_im, p)
    q = rope_partial(q.reshape(n_b, n_t, N_HEADS, HEAD_DIM), pos)
    k = rope_partial(k.reshape(n_b, n_t, KV_HEADS, HEAD_DIM), pos)
    v = v.reshape(n_b, n_t, KV_HEADS, HEAD_DIM)
    qi = rope_partial(qi.reshape(n_b, n_t, IDX_HEADS, IDX_DIM), pos)
    ki = rope_partial(ki[:, :, None, :], pos)[:, :, 0]
    y_b = attend_fn(q, k, v, qi, ki, wi)
    mixed = jax.nn.sigmoid(ga) * (y_a @ p['p_a']) + jax.nn.sigmoid(gb) * (y_b @ p['p_b'])
    x = x + mixed @ p['w_o']
    h2 = rms_norm(x, p['g_ffn'])
    x = x + (jax.nn.silu(h2 @ p['w_gate']) * (h2 @ p['w_up'])) @ p['w_down']
    return x, k, v, ki, hT_re, hT_im


def setup_inputs(seed: int = 0) -> dict:
    key = jax.random.key(seed)
    ks = iter(jax.random.split(key, 40))
    f32 = jnp.float32

    def nrm(shape, scale):
        return jax.random.normal(next(ks), shape, f32) * scale

    n_pages = PAST_LEN // PAGE_SIZE
    n_used = DEC_BATCH * n_pages
    n_phys = n_used + max(1, n_used // 4)
    x_prompt = nrm((BATCH, SEQ, D_MODEL), 1.0)
    x_sample = nrm((DEC_BATCH, DEC_SEQ, D_MODEL), 1.0)
    cache_k = nrm((DEPTH, n_phys, PAGE_SIZE, KV_HEADS, HEAD_DIM), 1.0)
    cache_v = nrm((DEPTH, n_phys, PAGE_SIZE, KV_HEADS, HEAD_DIM), 1.0)
    cache_kidx = nrm((DEPTH, n_phys, PAGE_SIZE, IDX_DIM), 1.0)
    state_ssm_re = nrm((DEPTH, DEC_BATCH, SSM_GROUPS, SSM_STATE), 0.1)
    state_ssm_im = nrm((DEPTH, DEC_BATCH, SSM_GROUPS, SSM_STATE), 0.1)
    page_table = jax.random.permutation(next(ks), n_phys)[:n_used].reshape(DEC_BATCH, n_pages).astype(jnp.int32)
    g_mix = 1.0 + nrm((DEPTH, D_MODEL), 0.02)
    w_in = nrm((DEPTH, D_MODEL, D_IN), D_MODEL ** -0.5)
    ssm_a_re = -0.5 + nrm((DEPTH, SSM_GROUPS, SSM_STATE), 0.01)
    ssm_a_im = math.pi * jnp.arange(SSM_STATE, dtype=f32) + nrm((DEPTH, SSM_GROUPS, SSM_STATE), 0.01)
    ssm_log_dt = jax.random.uniform(next(ks), (DEPTH, SSM_GROUPS), f32, math.log(1e-3), math.log(1e-1))
    ssm_b_re = nrm((DEPTH, SSM_GROUPS, SSM_STATE, SSM_GROUP_CH), (2 * SSM_GROUP_CH) ** -0.5)
    ssm_b_im = nrm((DEPTH, SSM_GROUPS, SSM_STATE, SSM_GROUP_CH), (2 * SSM_GROUP_CH) ** -0.5)
    ssm_c_re = nrm((DEPTH, SSM_GROUPS, SSM_GROUP_CH, SSM_STATE), SSM_STATE ** -0.5)
    ssm_c_im = nrm((DEPTH, SSM_GROUPS, SSM_GROUP_CH, SSM_STATE), SSM_STATE ** -0.5)
    ssm_d = 1.0 + nrm((DEPTH, D_SSM), 0.1)
    ssm_w_glu = nrm((DEPTH, D_SSM, D_SSM), D_SSM ** -0.5)
    p_a = nrm((DEPTH, D_SSM, D_MODEL), D_SSM ** -0.5)
    p_b = nrm((DEPTH, D_ATT, D_MODEL), D_ATT ** -0.5)
    w_o = nrm((DEPTH, D_MODEL, D_MODEL), D_MODEL ** -0.5)
    g_ffn = 1.0 + nrm((DEPTH, D_MODEL), 0.02)
    w_gate = nrm((DEPTH, D_MODEL, D_FF), D_MODEL ** -0.5)
    w_up = nrm((DEPTH, D_MODEL, D_FF), D_MODEL ** -0.5)
    w_down = nrm((DEPTH, D_FF, D_MODEL), D_FF ** -0.5)
    g_final = 1.0 + nrm((D_MODEL,), 0.02)
    return {'x_prompt': x_prompt, 'x_sample': x_sample, 'cache_k': cache_k, 'cache_v': cache_v,
            'cache_kidx': cache_kidx, 'state_ssm_re': state_ssm_re, 'state_ssm_im': state_ssm_im,
            'page_table': page_table, 'g_mix': g_mix, 'w_in': w_in, 'ssm_a_re': ssm_a_re,
            'ssm_a_im': ssm_a_im, 'ssm_log_dt': ssm_log_dt, 'ssm_b_re': ssm_b_re, 'ssm_b_im': ssm_b_im,
            'ssm_c_re': ssm_c_re, 'ssm_c_im': ssm_c_im, 'ssm_d': ssm_d, 'ssm_w_glu': ssm_w_glu,
            'p_a': p_a, 'p_b': p_b, 'w_o': w_o, 'g_ffn': g_ffn, 'w_gate': w_gate, 'w_up': w_up,
            'w_down': w_down, 'g_final': g_final}


def reference(x_prompt, x_sample, cache_k, cache_v, cache_kidx, state_ssm_re, state_ssm_im, page_table,
              g_mix, w_in, ssm_a_re, ssm_a_im, ssm_log_dt, ssm_b_re, ssm_b_im, ssm_c_re, ssm_c_im,
              ssm_d, ssm_w_glu, p_a, p_b, w_o, g_ffn, w_gate, w_up, w_down, g_final):
    n_b, n_s = x_prompt.shape[:2]
    n_new = x_sample.shape[1]
    past = page_table.shape[1] * PAGE_SIZE
    pos_p = jnp.arange(n_s)
    pos_s = past + jnp.arange(n_new)
    xp, xs = x_prompt, x_sample
    kp, vp, kip, hrp, hip = [], [], [], [], []
    ksm, vsm, kism, hrs, his = [], [], [], [], []
    for l in range(DEPTH):
        p = {'g_mix': g_mix[l], 'w_in': w_in[l], 'a_re': ssm_a_re[l], 'a_im': ssm_a_im[l],
             'log_dt': ssm_log_dt[l], 'b_re': ssm_b_re[l], 'b_im': ssm_b_im[l], 'c_re': ssm_c_re[l],
             'c_im': ssm_c_im[l], 'd': ssm_d[l], 'w_glu': ssm_w_glu[l], 'p_a': p_a[l], 'p_b': p_b[l],
             'w_o': w_o[l], 'g_ffn': g_ffn[l], 'w_gate': w_gate[l], 'w_up': w_up[l], 'w_down': w_down[l]}
        zeros = jnp.zeros((n_b, SSM_GROUPS, SSM_STATE), x_prompt.dtype)
        xp, k1, v1, ki1, hr1, hi1 = trunk_layer(xp, pos_p, zeros, zeros, prompt_attention, p)
        samp_attn = functools.partial(sample_attention, cache_k, cache_v, cache_kidx, page_table, l)
        xs, k2, v2, ki2, hr2, hi2 = trunk_layer(xs, pos_s, state_ssm_re[l], state_ssm_im[l], samp_attn, p)
        kp.append(k1); vp.append(v1); kip.append(ki1); hrp.append(hr1); hip.append(hi1)
        ksm.append(k2); vsm.append(v2); kism.append(ki2); hrs.append(hr2); his.append(hi2)
    y_prompt = rms_norm(xp, g_final)
    y_sample = rms_norm(xs, g_final)
    return (y_prompt, y_sample,
            jnp.stack(kp), jnp.stack(vp), jnp.stack(kip), jnp.stack(hrp), jnp.stack(hip),
            jnp.stack(ksm), jnp.stack(vsm), jnp.stack(kism), jnp.stack(hrs), jnp.stack(his))
```

```python
import functools
import math

import jax
import jax.numpy as jnp
import numpy as np
from jax import lax
from jax.experimental import pallas as pl
from jax.experimental.pallas import tpu as pltpu

F32 = jnp.float32
BF16 = jnp.bfloat16
I32 = jnp.int32

EPS = 1e-6
PAGE_SIZE = 128
SSM_GROUP_CH = 16
SSM_STATE = 64
N_HEADS = 8
HEAD_DIM = 64
KV_HEADS = 2
ROT_DIM = HEAD_DIM // 4
ROPE_THETA = 500000.0
IDX_HEADS = 8
IDX_DIM = 64
TOPK_MAX = 256

LANES = 128
SUBLANES = 8
VMEM_LIMIT = 56 * 1024 * 1024

INT_MIN = -(2 ** 31)
INT_MAX = 2 ** 31 - 1
NEG_MASKED = -2e30
NEG_INIT = -1e30

D_MODEL = 1024
D_SSM = 512
D_ATT = 512
COL_U = 0
COL_Q = 512
COL_GA = 1024
COL_GB = 2048
COL_QI = 3072
COL_K = 3584
COL_V = 3712
COL_KW = 3840
D_IN_PAD = 3968

SAMPLE_PAGES_PER_STEP = 8


def _cparams(sem):
    return pltpu.CompilerParams(dimension_semantics=sem, vmem_limit_bytes=VMEM_LIMIT)


def _sigmoid(x):
    return 1.0 / (1.0 + jnp.exp(-x))


def _rms(x, g):
    return x * lax.rsqrt(jnp.mean(x * x, axis=-1, keepdims=True) + EPS) * g


def _norm_proj_kernel(x_ref, g_ref, w_ref, o_ref, *, col_chunk):
    h = _rms(x_ref[...], g_ref[...]).astype(BF16)
    n = o_ref.shape[1]
    for c0 in range(0, n, col_chunk):
        c1 = min(n, c0 + col_chunk)
        o_ref[:, c0:c1] = jnp.dot(h, w_ref[:, c0:c1], preferred_element_type=F32)


def norm_proj(x, g, w_bf):
    m, d = x.shape
    n = w_bf.shape[1]
    tm = min(256, m)
    return pl.pallas_call(
        functools.partial(_norm_proj_kernel, col_chunk=512),
        grid=(m // tm,),
        in_specs=[pl.BlockSpec((tm, d), lambda i: (i, 0)),
                  pl.BlockSpec((1, d), lambda i: (0, 0)),
                  pl.BlockSpec((d, n), lambda i: (0, 0))],
        out_specs=pl.BlockSpec((tm, n), lambda i: (i, 0)),
        out_shape=jax.ShapeDtypeStruct((m, n), F32),
        compiler_params=_cparams(("parallel",)),
        name="norm_proj",
    )(x, g.reshape(1, d), w_bf)


def _rope_kernel(q_ref, qi_ref, k_ref, v_ref, kw_ref, tab_ref,
                 qz_ref, qir_ref, kr_ref, kbf_ref, vo_ref, kir_ref, kcat_ref, *t_refs, transposed):
    c = tab_ref[0]
    s1 = tab_ref[1]
    s2 = tab_ref[2]

    def rope(x):
        return x * c + pltpu.roll(x, LANES - ROT_DIM // 2, 1) * s1 + pltpu.roll(x, ROT_DIM // 2, 1) * s2

    lane = lax.broadcasted_iota(I32, c.shape, 1)
    lo_half = lane < HEAD_DIM
    heads_per_blk = LANES // HEAD_DIM
    q_rep = N_HEADS // KV_HEADS
    for blk in range(D_ATT // LANES):
        sl = slice(blk * LANES, (blk + 1) * LANES)
        qir_ref[:, sl] = rope(qi_ref[:, sl])
        x = rope(q_ref[:, sl]) * (HEAD_DIM ** -0.5)
        xr = pltpu.roll(x, HEAD_DIM, 1)
        for par in range(heads_per_blk):
            h = blk * heads_per_blk + par
            g = h // q_rep
            src = x if par == g else xr
            keep = lo_half if g == 0 else jnp.logical_not(lo_half)
            qz_ref[:, h * LANES:(h + 1) * LANES] = jnp.where(keep, src, 0.0)
    kr = rope(k_ref[...])
    kr_ref[...] = kr
    kbf_ref[...] = kr.astype(BF16)
    v = v_ref[...]
    vo_ref[...] = v
    kw = kw_ref[...]
    kwr = rope(kw)
    kir_ref[...] = kwr[:, :IDX_DIM]
    kd = jnp.where(lo_half, kwr, pltpu.roll(kwr, IDX_DIM, 1))
    hi = kd.astype(BF16)
    lo = kd - hi.astype(F32)
    kcat_ref[:, :LANES] = hi
    kcat_ref[:, LANES:] = jnp.where(lo_half, lo, 0.0).astype(BF16)
    if transposed:
        vt_ref, wt_ref = t_refs
        vt_ref[0] = v.T.astype(BF16)
        wt_ref[...] = kw.T[IDX_DIM:IDX_DIM + IDX_HEADS, :]


def rope_prep(z, tab, tt, transposed):
    m = z.shape[0]
    n_t = tab.shape[1] // tt
    row = lambda i: i
    in_specs = [pl.BlockSpec((tt, D_ATT), lambda i: (row(i), COL_Q // D_ATT)),
                pl.BlockSpec((tt, D_ATT), lambda i: (row(i), COL_QI // D_ATT)),
                pl.BlockSpec((tt, LANES), lambda i: (row(i), COL_K // LANES)),
                pl.BlockSpec((tt, LANES), lambda i: (row(i), COL_V // LANES)),
                pl.BlockSpec((tt, LANES), lambda i: (row(i), COL_KW // LANES)),
                pl.BlockSpec((3, tt, LANES), lambda i: (0, i % n_t, 0))]
    out_specs = [pl.BlockSpec((tt, N_HEADS * LANES), lambda i: (i, 0)),
                 pl.BlockSpec((tt, D_ATT), lambda i: (i, 0)),
                 pl.BlockSpec((tt, LANES), lambda i: (i, 0)),
                 pl.BlockSpec((tt, LANES), lambda i: (i, 0)),
                 pl.BlockSpec((tt, LANES), lambda i: (i, 0)),
                 pl.BlockSpec((tt, IDX_DIM), lambda i: (i, 0)),
                 pl.BlockSpec((tt, 2 * LANES), lambda i: (i, 0))]
    out_shape = [jax.ShapeDtypeStruct((m, N_HEADS * LANES), F32),
                 jax.ShapeDtypeStruct((m, D_ATT), F32),
                 jax.ShapeDtypeStruct((m, LANES), F32),
                 jax.ShapeDtypeStruct((m, LANES), BF16),
                 jax.ShapeDtypeStruct((m, LANES), F32),
                 jax.ShapeDtypeStruct((m, IDX_DIM), F32),
                 jax.ShapeDtypeStruct((m, 2 * LANES), BF16)]
    if transposed:
        out_specs += [pl.BlockSpec((1, LANES, tt), lambda i: (i, 0, 0)),
                      pl.BlockSpec((IDX_HEADS, tt), lambda i: (0, i))]
        out_shape += [jax.ShapeDtypeStruct((m // tt, LANES, tt), BF16),
                      jax.ShapeDtypeStruct((IDX_HEADS, m), F32)]
    return pl.pallas_call(
        functools.partial(_rope_kernel, transposed=transposed),
        grid=(m // tt,),
        in_specs=in_specs, out_specs=out_specs, out_shape=out_shape,
        compiler_params=_cparams(("parallel",)),
        name="rope_prep",
    )(z, z, z, z, z, tab)


def rope_tables(pos):
    half = ROT_DIM // 2
    freqs = ROPE_THETA ** (-jnp.arange(half, dtype=F32) / half)
    ang = pos.astype(F32)[:, None] * freqs[None, :]
    cos, sin = jnp.cos(ang), jnp.sin(ang)
    n = pos.shape[0]
    one = jnp.ones((n, HEAD_DIM - ROT_DIM), F32)
    zero = jnp.zeros((n, HEAD_DIM - ROT_DIM), F32)
    zh = jnp.zeros((n, half), F32)
    c = jnp.concatenate([cos, cos, one], axis=1)
    s1 = jnp.concatenate([-sin, zh, zero], axis=1)
    s2 = jnp.concatenate([zh, sin, zero], axis=1)
    tab = jnp.stack([c, s1, s2])
    return jnp.tile(tab, (1, 1, LANES // HEAD_DIM))


def _gelu_tanh(x):
    return 0.5 * x * (1.0 + jnp.tanh(math.sqrt(2.0 / math.pi) * (x + 0.044715 * (x * x * x))))


def _s5_kernel(u_ref, h0_ref, wb_ref, cst_ref, wc_ref, d_ref, wglu_ref, y_ref, ht_ref,
               bb_scr, carry_scr, *, independent_slabs):
    tc = u_ref.shape[0]
    gp = wb_ref.shape[1] // 2
    n_slab = tc // SUBLANES
    u = u_ref[...]
    bb_scr[...] = jnp.dot(u.astype(BF16), wb_ref[...], preferred_element_type=F32)

    if not independent_slabs:
        @pl.when(pl.program_id(1) == 0)
        def _():
            carry_scr[...] = h0_ref[0]

    for col in range(gp // LANES):
        cre = slice(col * LANES, (col + 1) * LANES)
        cim = slice(gp + col * LANES, gp + (col + 1) * LANES)
        steps = [(1, cst_ref[0, :, cre], cst_ref[1, :, cre]),
                 (2, cst_ref[2, :, cre], cst_ref[3, :, cre]),
                 (4, cst_ref[4, :, cre], cst_ref[5, :, cre])]
        p_re = cst_ref[6, :, cre]
        p_im = cst_ref[7, :, cre]

        def body(r, carry):
            rows = pl.ds(pl.multiple_of(r * SUBLANES, SUBLANES), SUBLANES)
            hr = bb_scr[rows, cre]
            hi = bb_scr[rows, cim]
            for d, a_re, a_im in steps:
                sr = pltpu.roll(hr, d, 0)
                si = pltpu.roll(hi, d, 0)
                hr, hi = hr + a_re * sr - a_im * si, hi + a_re * si + a_im * sr
            if independent_slabs:
                c_re = h0_ref[r, :, cre]
                c_im = h0_ref[r, :, cim]
            else:
                c_re, c_im = carry
            hr, hi = hr + p_re * c_re - p_im * c_im, hi + p_re * c_im + p_im * c_re
            bb_scr[rows, cre] = hr
            bb_scr[rows, cim] = hi
            last_re = hr[SUBLANES - 1:SUBLANES, :]
            last_im = hi[SUBLANES - 1:SUBLANES, :]
            if independent_slabs:
                ht_ref[r, :, cre] = last_re
                ht_ref[r, :, cim] = last_im
                return carry
            return last_re, last_im

        if independent_slabs:
            lax.fori_loop(0, n_slab, body, 0)
        else:
            out_re, out_im = lax.fori_loop(0, n_slab, body, (carry_scr[:, cre], carry_scr[:, cim]))
            carry_scr[:, cre] = out_re
            carry_scr[:, cim] = out_im

    if not independent_slabs:
        ht_ref[0] = carry_scr[...]
    y = jnp.dot(bb_scr[...].astype(BF16), wc_ref[...], preferred_element_type=F32) + d_ref[...] * u
    g = _gelu_tanh(y)
    gate = jnp.dot(g.astype(BF16), wglu_ref[...], preferred_element_type=F32)
    y_ref[...] = (g * _sigmoid(gate)).astype(BF16)


def s5_branch(z, h0, sp, n_b, n_t, independent_slabs):
    m = z.shape[0]
    gp2 = sp['wb'].shape[1]
    const = lambda *shape: pl.BlockSpec(shape, lambda *_: (0,) * len(shape))
    w_specs = [const(D_SSM, gp2), const(8, SUBLANES, gp2 // 2), const(gp2, D_SSM),
               const(1, D_SSM), const(D_SSM, D_SSM)]
    if independent_slabs:
        assert n_t == SUBLANES
        tc = m
        grid = (1,)
        in_specs = [pl.BlockSpec((tc, D_SSM), lambda i: (0, COL_U // D_SSM)),
                    pl.BlockSpec((n_b, 1, gp2), lambda i: (0, 0, 0))] + w_specs
        out_specs = [pl.BlockSpec((tc, D_SSM), lambda i: (0, 0)),
                     pl.BlockSpec((n_b, 1, gp2), lambda i: (0, 0, 0))]
        ht_shape = (n_b, 1, gp2)
        h0_in = h0.reshape(n_b, 1, gp2)
        sem = ("arbitrary",)
    else:
        tc = 256
        n_c = n_t // tc
        grid = (n_b, n_c)
        in_specs = [pl.BlockSpec((tc, D_SSM), lambda b, j: (b * n_c + j, COL_U // D_SSM)),
                    pl.BlockSpec((1, 1, gp2), lambda b, j: (b, 0, 0))] + w_specs
        out_specs = [pl.BlockSpec((tc, D_SSM), lambda b, j: (b * n_c + j, 0)),
                     pl.BlockSpec((1, 1, gp2), lambda b, j: (b, 0, 0))]
        ht_shape = (n_b, 1, gp2)
        h0_in = h0.reshape(n_b, 1, gp2)
        sem = ("parallel", "arbitrary")
    y, ht = pl.pallas_call(
        functools.partial(_s5_kernel, independent_slabs=independent_slabs),
        grid=grid, in_specs=in_specs, out_specs=out_specs,
        out_shape=[jax.ShapeDtypeStruct((m, D_SSM), BF16), jax.ShapeDtypeStruct(ht_shape, F32)],
        scratch_shapes=[pltpu.VMEM((tc, gp2), F32), pltpu.VMEM((1, gp2), F32)],
        compiler_params=_cparams(sem),
        name="s5_branch",
    )(z, h0_in, sp['wb'], sp['cst'], sp['wc'], sp['d'], sp['wglu'])
    return y, ht.reshape(n_b, gp2)


def s5_params(a_re, a_im, log_dt, b_re, b_im, c_re, c_im, d, w_glu):
    n_g, n_p = a_re.shape
    dt = jnp.exp(log_dt)[:, None]
    mag = jnp.exp(dt * a_re)
    abar_re = mag * jnp.cos(dt * a_im)
    abar_im = mag * jnp.sin(dt * a_im)
    nr, ni = abar_re - 1.0, abar_im
    den = a_re * a_re + a_im * a_im
    coef_re = (nr * a_re + ni * a_im) / den
    coef_im = (ni * a_re - nr * a_im) / den
    bbar_re = coef_re[:, :, None] * b_re - coef_im[:, :, None] * b_im
    bbar_im = coef_re[:, :, None] * b_im + coef_im[:, :, None] * b_re
    eye = jnp.eye(n_g, dtype=F32)
    gp = n_g * n_p
    d_ssm = n_g * SSM_GROUP_CH
    blk_b = lambda t: jnp.einsum('gpc,gh->gchp', t, eye).reshape(d_ssm, gp)
    blk_c = lambda t: jnp.einsum('gcp,gh->gphc', t, eye).reshape(gp, d_ssm)
    wb = jnp.concatenate([blk_b(bbar_re), blk_b(bbar_im)], axis=1).astype(BF16)
    wc = jnp.concatenate([blk_c(c_re), -blk_c(c_im)], axis=0).astype(BF16)
    ar, ai = abar_re.reshape(gp), abar_im.reshape(gp)
    pows = [(ar, ai)]
    for _ in range(SUBLANES - 1):
        pr, pi = pows[-1]
        pows.append((pr * ar - pi * ai, pr * ai + pi * ar))
    rows = jnp.arange(SUBLANES)[:, None]
    cst = []
    for dd in (1, 2, 4):
        pr, pi = pows[dd - 1]
        cst.append(jnp.where(rows >= dd, pr[None, :], 0.0))
        cst.append(jnp.where(rows >= dd, pi[None, :], 0.0))
    cst.append(jnp.stack([p[0] for p in pows]))
    cst.append(jnp.stack([p[1] for p in pows]))
    return {'wb': wb, 'wc': wc, 'cst': jnp.stack(cst), 'd': d.reshape(1, d_ssm),
            'wglu': w_glu.astype(BF16)}


def _score_key(score):
    bits = pltpu.bitcast(score, I32)
    return bits ^ ((bits >> 31) & INT_MAX)


def _bisect_threshold(count_ge, topk, like):
    cnt0 = count_ge(jnp.zeros_like(like))
    prefix0 = jnp.where(cnt0 >= topk, 0, INT_MIN).astype(I32)

    def body(b, prefix):
        cand = prefix | lax.shift_left(jnp.int32(1), 30 - b)
        return jnp.where(count_ge(cand) >= topk, cand, prefix)

    return lax.fori_loop(0, 31, body, prefix0)


def _bisect_tie_index(count_eq_below, need, n_bits, like):
    def body(b, prefix):
        cand = prefix | lax.shift_left(jnp.int32(1), n_bits - 1 - b)
        return jnp.where(count_eq_below(cand) < need, cand, prefix)

    return lax.fori_loop(0, n_bits, body, jnp.zeros_like(like))


def _prompt_attn_kernel(qi_ref, wt_ref, qz_ref, kcat_ref, kbf_ref, vt_ref, o_ref,
                        qcat_scr, qzs_scr, keys_scr, tau_scr, jst_scr, m_scr, l_scr, acc_scr,
                        *, topk, idx_bits):
    tq = qi_ref.shape[0]
    ts = vt_ref.shape[3]
    j = pl.program_id(1)
    n_tiles = (j * tq + tq + ts - 1) // ts

    lane = lax.broadcasted_iota(I32, (tq, LANES), 1)
    lo_half = lane < IDX_DIM
    for blk in range(IDX_HEADS * IDX_DIM // LANES):
        x = qi_ref[:, blk * LANES:(blk + 1) * LANES]
        xr = pltpu.roll(x, IDX_DIM, 1)
        for par in range(2):
            h = 2 * blk + par
            dup = jnp.where(lo_half, x, xr) if par == 0 else jnp.where(lo_half, xr, x)
            hi = dup.astype(BF16).astype(F32)
            lh = jnp.where(lo_half, hi, dup - hi).astype(BF16)
            qcat_scr[h * tq:(h + 1) * tq, :LANES] = lh
            qcat_scr[h * tq:(h + 1) * tq, LANES:] = lh
    for h in range(N_HEADS):
        qzs_scr[h * tq:(h + 1) * tq, :] = qz_ref[:, h * LANES:(h + 1) * LANES].astype(BF16)

    wt = wt_ref[...]
    tcol = j * tq + lax.broadcasted_iota(I32, (ts, tq), 1)
    srow0 = lax.broadcasted_iota(I32, (ts, tq), 0)
    nt_dims = (((1,), (1,)), ((), ()))

    def idx_body(i, _):
        s0 = pl.multiple_of(i * ts, ts)
        dt = lax.dot_general(kcat_ref[0, pl.ds(s0, ts), :], qcat_scr[...], nt_dims,
                             preferred_element_type=F32)
        sc = jnp.zeros((ts, tq), F32)
        for h in range(IDX_HEADS):
            sc = sc + wt[h:h + 1, :] * jnp.maximum(dt[:, h * tq:(h + 1) * tq], 0.0)
        key = jnp.where(srow0 + s0 <= tcol, _score_key(sc), INT_MIN)
        keys_scr[pl.ds(s0, ts), :] = key
        return 0

    lax.fori_loop(0, n_tiles, idx_body, 0)

    def count(pred):
        def body(i, acc):
            s0 = pl.multiple_of(i * ts, ts)
            hit = jnp.where(pred(keys_scr[pl.ds(s0, ts), :], s0), 1.0, 0.0)
            return acc + jnp.sum(hit.reshape(ts // SUBLANES, SUBLANES, tq), axis=0)
        acc = lax.fori_loop(0, n_tiles, body, jnp.zeros((SUBLANES, tq), F32))
        return jnp.sum(acc, axis=0, keepdims=True)

    like = jnp.zeros((1, tq), I32)
    kf = float(topk)
    tau = _bisect_threshold(lambda cand: count(lambda k, s0: k >= cand), kf, like)
    tau = jnp.maximum(tau, INT_MIN + 1)
    cnt_gt = count(lambda k, s0: k > tau)
    cnt_ge = count(lambda k, s0: k >= tau)
    need = kf - cnt_gt
    tie = (cnt_ge - cnt_gt) > need
    tau_scr[...] = tau
    jst_scr[...] = jnp.full((1, tq), INT_MAX, I32)

    @pl.when(jnp.max(jnp.where(tie, 1.0, 0.0)) > 0.0)
    def _():
        jst = _bisect_tie_index(
            lambda cand: count(lambda k, s0: (k == tau) & (srow0 + s0 < cand)), need, idx_bits, like)
        jst_scr[...] = jnp.where(tie, jst, INT_MAX)

    jst = jst_scr[...]
    m_scr[...] = jnp.full(m_scr.shape, NEG_INIT, F32)
    l_scr[...] = jnp.zeros(l_scr.shape, F32)
    acc_scr[...] = jnp.zeros(acc_scr.shape, F32)

    def att_body(i, _):
        s0 = pl.multiple_of(i * ts, ts)
        lt = lax.dot_general(kbf_ref[0, pl.ds(s0, ts), :], qzs_scr[...], nt_dims,
                             preferred_element_type=F32)
        key = keys_scr[pl.ds(s0, ts), :]
        sel = (key > tau) | ((key == tau) & (srow0 + s0 <= jst))
        lt = jnp.concatenate(
            [jnp.where(sel, lt[:, h * tq:(h + 1) * tq], NEG_MASKED) for h in range(N_HEADS)], axis=1)
        m_old = m_scr[...]
        m_new = jnp.maximum(m_old, jnp.max(lt, axis=0, keepdims=True))
        alpha = jnp.exp(m_old - m_new)
        p = jnp.exp(lt - m_new)
        l_scr[...] = alpha * l_scr[...] + jnp.sum(p, axis=0, keepdims=True)
        acc_scr[...] = alpha * acc_scr[...] + jnp.dot(vt_ref[0, i], p.astype(BF16),
                                                      preferred_element_type=F32)
        m_scr[...] = m_new
        return 0

    lax.fori_loop(0, n_tiles, att_body, 0)

    accn = acc_scr[...] / l_scr[...]
    q_rep = N_HEADS // KV_HEADS
    for blk in range(D_ATT // LANES):
        h0 = 2 * blk
        g = h0 // q_rep
        rows = slice(g * HEAD_DIM, (g + 1) * HEAD_DIM)
        pair = jnp.concatenate([accn[rows, h0 * tq:(h0 + 1) * tq],
                                accn[rows, (h0 + 1) * tq:(h0 + 2) * tq]], axis=0)
        o_ref[:, blk * LANES:(blk + 1) * LANES] = pair.T.astype(BF16)


def prompt_attention(qi_r, wt, qz, kcat, kbf, vt, n_b, n_t, tq, ts):
    m = n_b * n_t
    n_q = n_t // tq
    topk = min(TOPK_MAX, n_t // 4)
    idx_bits = max(1, int(math.ceil(math.log2(n_t))))
    return pl.pallas_call(
        functools.partial(_prompt_attn_kernel, topk=topk, idx_bits=idx_bits),
        grid=(n_b, n_q),
        in_specs=[pl.BlockSpec((tq, D_ATT), lambda b, j: (b * n_q + j, 0)),
                  pl.BlockSpec((IDX_HEADS, tq), lambda b, j: (0, b * n_q + j)),
                  pl.BlockSpec((tq, N_HEADS * LANES), lambda b, j: (b * n_q + j, 0)),
                  pl.BlockSpec((1, n_t, 2 * LANES), lambda b, j: (b, 0, 0)),
                  pl.BlockSpec((1, n_t, LANES), lambda b, j: (b, 0, 0)),
                  pl.BlockSpec((1, n_t // ts, LANES, ts), lambda b, j: (b, 0, 0, 0))],
        out_specs=pl.BlockSpec((tq, D_ATT), lambda b, j: (b * n_q + j, 0)),
        out_shape=jax.ShapeDtypeStruct((m, D_ATT), BF16),
        scratch_shapes=[pltpu.VMEM((IDX_HEADS * tq, 2 * LANES), BF16),
                        pltpu.VMEM((N_HEADS * tq, LANES), BF16),
                        pltpu.VMEM((n_t, tq), I32),
                        pltpu.VMEM((1, tq), I32),
                        pltpu.VMEM((1, tq), I32),
                        pltpu.VMEM((1, N_HEADS * tq), F32),
                        pltpu.VMEM((1, N_HEADS * tq), F32),
                        pltpu.VMEM((LANES, N_HEADS * tq), F32)],
        compiler_params=_cparams(("parallel", "arbitrary")),
        name="prompt_attention",
    )(qi_r, wt, qz, kcat.reshape(n_b, n_t, 2 * LANES), kbf.reshape(n_b, n_t, LANES),
      vt.reshape(n_b, n_t // ts, LANES, ts))


def _sample_attn_kernel(pt_ref, qi_ref, kw_ref, qz_ref, kin_ref, kn_ref, vn_ref, *rest,
                        n_pages_step, n_chunks, topk, idx_bits):
    del pt_ref
    p = n_pages_step
    kidx_refs = rest[:p]
    k_refs = rest[p:2 * p]
    v_refs = rest[2 * p:3 * p]
    o_ref = rest[3 * p]
    qhl_scr, qzs_scr, keys_scr, tau_scr, jst_scr, m_scr, l_scr, acc_scr = rest[3 * p + 1:]
    n_q = qi_ref.shape[1]
    ck = p * PAGE_SIZE
    j = pl.program_id(1)
    nt_dims = (((1,), (1,)), ((), ()))
    n_rows = IDX_HEADS * n_q
    w_all = kw_ref[0]
    qpos = n_chunks * ck + lax.broadcasted_iota(I32, (n_q, ck), 0)
    lane_c = lax.broadcasted_iota(I32, (n_q, ck), 1)

    @pl.when(j == 0)
    def _():
        qi = qi_ref[0]
        q_all = jnp.concatenate([qi[:, h * IDX_DIM:(h + 1) * IDX_DIM] for h in range(IDX_HEADS)],
                                axis=0)
        hi = q_all.astype(BF16)
        qhl_scr[:n_rows, :] = hi
        qhl_scr[n_rows:, :] = (q_all - hi.astype(F32)).astype(BF16)
        qz = qz_ref[0]
        for h in range(N_HEADS):
            qzs_scr[h * n_q:(h + 1) * n_q, :] = qz[:, h * LANES:(h + 1) * LANES]

    def scores(keys_f32):
        khi = keys_f32.astype(BF16)
        klo = (keys_f32 - khi.astype(F32)).astype(BF16)
        r1 = lax.dot_general(qhl_scr[...], khi, nt_dims, preferred_element_type=F32)
        r2 = lax.dot_general(qhl_scr[:n_rows, :], klo, nt_dims, preferred_element_type=F32)
        dots = r1[:n_rows] + r1[n_rows:] + r2
        sc = jnp.zeros((n_q, keys_f32.shape[0]), F32)
        for h in range(IDX_HEADS):
            sc = sc + w_all[:, IDX_DIM + h:IDX_DIM + h + 1] * jnp.maximum(dots[h * n_q:(h + 1) * n_q], 0.0)
        return sc

    @pl.when(j < n_chunks)
    def _():
        kc = jnp.concatenate([r[0, 0] for r in kidx_refs], axis=0)
        keys_scr[j] = _score_key(scores(kc))

    def count(pred):
        def body(i, acc):
            blk = keys_scr[i]
            hit = jnp.where(pred(blk, i * ck), 1.0, 0.0)
            part = hit[:, :LANES]
            for c in range(1, ck // LANES):
                part = part + hit[:, c * LANES:(c + 1) * LANES]
            return acc + part
        acc = lax.fori_loop(0, n_chunks + 1, body, jnp.zeros((n_q, LANES), F32))
        return jnp.sum(acc, axis=1, keepdims=True)

    @pl.when(j == n_chunks - 1)
    def _():
        sc_new = scores(kin_ref[0])
        key_new = jnp.where(lane_c[:, :LANES] <= qpos[:, :LANES] - n_chunks * ck,
                            _score_key(sc_new), INT_MIN)
        keys_scr[n_chunks] = jnp.concatenate(
            [key_new, jnp.full((n_q, ck - LANES), INT_MIN, I32)], axis=1)
        like = jnp.zeros((n_q, 1), I32)
        kf = float(topk)
        tau = _bisect_threshold(lambda cand: count(lambda k, s0: k >= cand), kf, like)
        tau = jnp.maximum(tau, INT_MIN + 1)
        cnt_gt = count(lambda k, s0: k > tau)
        cnt_ge = count(lambda k, s0: k >= tau)
        need = kf - cnt_gt
        tie = (cnt_ge - cnt_gt) > need
        tau_scr[...] = tau
        jst_scr[...] = jnp.full((n_q, 1), INT_MAX, I32)

        @pl.when(jnp.max(jnp.where(tie, 1.0, 0.0)) > 0.0)
        def _():
            jst = _bisect_tie_index(
                lambda cand: count(lambda k, s0: (k == tau) & (lane_c + s0 < cand)), need, idx_bits, like)
            jst_scr[...] = jnp.where(tie, jst, INT_MAX)

    def attend(kt, vt, key, s0):
        n = kt.shape[0]
        tau = tau_scr[...]
        sel = (key > tau) | ((key == tau) & (lane_c[:, :n] + s0 <= jst_scr[...]))
        lt = lax.dot_general(qzs_scr[...].astype(BF16), kt, nt_dims, preferred_element_type=F32)
        lt = jnp.concatenate(
            [jnp.where(sel, lt[h * n_q:(h + 1) * n_q], NEG_MASKED) for h in range(N_HEADS)], axis=0)
        m_old = m_scr[...]
        m_new = jnp.maximum(m_old, jnp.max(lt, axis=1, keepdims=True))
        alpha = jnp.exp(m_old - m_new)
        pr = jnp.exp(lt - m_new)
        l_scr[...] = alpha * l_scr[...] + jnp.sum(pr, axis=1, keepdims=True)
        acc_scr[...] = alpha * acc_scr[...] + jnp.dot(pr.astype(BF16), vt, preferred_element_type=F32)
        m_scr[...] = m_new

    @pl.when(j == n_chunks)
    def _():
        m_scr[...] = jnp.full(m_scr.shape, NEG_INIT, F32)
        l_scr[...] = jnp.zeros(l_scr.shape, F32)
        acc_scr[...] = jnp.zeros(acc_scr.shape, F32)
        attend(kn_ref[0].astype(BF16), vn_ref[0].astype(BF16), keys_scr[n_chunks][:, :LANES], n_chunks * ck)

    @pl.when(j >= n_chunks)
    def _():
        kt = jnp.concatenate([r[0, 0] for r in k_refs], axis=0).astype(BF16)
        vt = jnp.concatenate([r[0, 0] for r in v_refs], axis=0).astype(BF16)
        i = j - n_chunks
        attend(kt, vt, keys_scr[i], i * ck)

    @pl.when(j == 2 * n_chunks - 1)
    def _():
        accn = acc_scr[...] / l_scr[...]
        lo_half = lax.broadcasted_iota(I32, (n_q, LANES), 1) < HEAD_DIM
        q_rep = N_HEADS // KV_HEADS
        for blk in range(D_ATT // LANES):
            a = accn[(2 * blk) * n_q:(2 * blk + 1) * n_q]
            b = accn[(2 * blk + 1) * n_q:(2 * blk + 2) * n_q]
            if (2 * blk) // q_rep == 0:
                out = jnp.where(lo_half, a, pltpu.roll(b, HEAD_DIM, 1))
            else:
                out = jnp.where(lo_half, pltpu.roll(a, HEAD_DIM, 1), b)
            o_ref[0, :, blk * LANES:(blk + 1) * LANES] = out.astype(BF16)


def sample_attention(layer, page_table, cache_kidx, cache_k2, cache_v2, qi_r, z, qz, ki_r, k_r, v, n_b, n_q):
    n_pages = page_table.shape[1]
    p = SAMPLE_PAGES_PER_STEP
    n_chunks = n_pages // p
    past = n_pages * PAGE_SIZE
    topk = min(TOPK_MAX, (past + n_q) // 4)
    idx_bits = int(math.ceil(math.log2((n_chunks + 1) * p * PAGE_SIZE)))
    pad = lambda a: jnp.pad(a.reshape(n_b, n_q, a.shape[-1]), ((0, 0), (0, PAGE_SIZE - n_q), (0, 0)))

    def page_spec(width, i, phase):
        if phase == 0:
            chunk = lambda j: jnp.minimum(j, n_chunks - 1)
        else:
            chunk = lambda j: jnp.maximum(j - n_chunks, 0)
        return pl.BlockSpec((1, 1, PAGE_SIZE, width),
                            lambda b, j, pt: (layer, pt[b, chunk(j) * p + i], 0, 0))

    per_b = lambda width: pl.BlockSpec((1, n_q, width), lambda b, j, pt: (b, 0, 0))
    per_b_pad = lambda width: pl.BlockSpec((1, PAGE_SIZE, width), lambda b, j, pt: (b, 0, 0))
    in_specs = ([per_b(D_ATT),
                 pl.BlockSpec((1, n_q, LANES), lambda b, j, pt: (b, 0, COL_KW // LANES)),
                 per_b(N_HEADS * LANES), per_b_pad(IDX_DIM), per_b_pad(LANES), per_b_pad(LANES)]
                + [page_spec(IDX_DIM, i, 0) for i in range(p)]
                + [page_spec(LANES, i, 1) for i in range(p)]
                + [page_spec(LANES, i, 1) for i in range(p)])
    n_rows = N_HEADS * n_q
    grid_spec = pltpu.PrefetchScalarGridSpec(
        num_scalar_prefetch=1, grid=(n_b, 2 * n_chunks),
        in_specs=in_specs,
        out_specs=pl.BlockSpec((1, n_q, D_ATT), lambda b, j, pt: (b, 0, 0)),
        scratch_shapes=[pltpu.VMEM((2 * n_rows, IDX_DIM), BF16),
                        pltpu.VMEM((n_rows, LANES), F32),
                        pltpu.VMEM((n_chunks + 1, n_q, p * PAGE_SIZE), I32),
                        pltpu.VMEM((n_q, 1), I32),
                        pltpu.VMEM((n_q, 1), I32),
                        pltpu.VMEM((n_rows, 1), F32),
                        pltpu.VMEM((n_rows, 1), F32),
                        pltpu.VMEM((n_rows, LANES), F32)])
    y = pl.pallas_call(
        functools.partial(_sample_attn_kernel, n_pages_step=p, n_chunks=n_chunks, topk=topk, idx_bits=idx_bits),
        grid_spec=grid_spec,
        out_shape=jax.ShapeDtypeStruct((n_b, n_q, D_ATT), BF16),
        compiler_params=_cparams(("parallel", "arbitrary")),
        name="sample_attention",
    )(page_table, qi_r.reshape(n_b, n_q, D_ATT), z.reshape(n_b, n_q, z.shape[-1]),
      qz.reshape(n_b, n_q, N_HEADS * LANES), pad(ki_r), pad(k_r), pad(v),
      *([cache_kidx] * p), *([cache_k2] * p), *([cache_v2] * p))
    return y.reshape(n_b * n_q, D_ATT)


def _merge_kernel(x_ref, ya_ref, yb_ref, ga_ref, gb_ref, pa_ref, pb_ref, wo_ref, o_ref):
    a = jnp.dot(ya_ref[...], pa_ref[...], preferred_element_type=F32)
    b = jnp.dot(yb_ref[...], pb_ref[...], preferred_element_type=F32)
    mixed = _sigmoid(ga_ref[...]) * a + _sigmoid(gb_ref[...]) * b
    o_ref[...] = x_ref[...] + jnp.dot(mixed.astype(BF16), wo_ref[...], preferred_element_type=F32)


def merge_proj(x, y_a, y_b, z, pa_bf, pb_bf, wo_bf):
    m, d = x.shape
    tm = min(256, m)
    const = lambda *shape: pl.BlockSpec(shape, lambda i: (0,) * len(shape))
    return pl.pallas_call(
        _merge_kernel,
        grid=(m // tm,),
        in_specs=[pl.BlockSpec((tm, d), lambda i: (i, 0)),
                  pl.BlockSpec((tm, D_SSM), lambda i: (i, 0)),
                  pl.BlockSpec((tm, D_ATT), lambda i: (i, 0)),
                  pl.BlockSpec((tm, d), lambda i: (i, COL_GA // d)),
                  pl.BlockSpec((tm, d), lambda i: (i, COL_GB // d)),
                  const(D_SSM, d), const(D_ATT, d), const(d, d)],
        out_specs=pl.BlockSpec((tm, d), lambda i: (i, 0)),
        out_shape=jax.ShapeDtypeStruct((m, d), F32),
        compiler_params=_cparams(("parallel",)),
        name="merge_proj",
    )(x, y_a, y_b, z, z, pa_bf, pb_bf, wo_bf)


def _ffn_kernel(x_ref, g_ref, wg_ref, wu_ref, wd_ref, gf_ref, o_ref, h_scr, acc_scr, *, final_norm):
    k = pl.program_id(1)

    @pl.when(k == 0)
    def _():
        h_scr[...] = _rms(x_ref[...], g_ref[...]).astype(BF16)
        acc_scr[...] = jnp.zeros(acc_scr.shape, F32)

    h = h_scr[...]
    gate = jnp.dot(h, wg_ref[...], preferred_element_type=F32)
    up = jnp.dot(h, wu_ref[...], preferred_element_type=F32)
    act = (gate * _sigmoid(gate) * up).astype(BF16)
    acc_scr[...] += jnp.dot(act, wd_ref[...], preferred_element_type=F32)

    @pl.when(k == pl.num_programs(1) - 1)
    def _():
        y = x_ref[...] + acc_scr[...]
        o_ref[...] = _rms(y, gf_ref[...]) if final_norm else y


def ffn(x, g, wg_bf, wu_bf, wd_bf, g_final, final_norm):
    m, d = x.shape
    d_ff = wg_bf.shape[1]
    tm = 1024 if m % 1024 == 0 else min(256, m)
    tf = 256
    return pl.pallas_call(
        functools.partial(_ffn_kernel, final_norm=final_norm),
        grid=(m // tm, d_ff // tf),
        in_specs=[pl.BlockSpec((tm, d), lambda i, k: (i, 0)),
                  pl.BlockSpec((1, d), lambda i, k: (0, 0)),
                  pl.BlockSpec((d, tf), lambda i, k: (0, k)),
                  pl.BlockSpec((d, tf), lambda i, k: (0, k)),
                  pl.BlockSpec((tf, d), lambda i, k: (k, 0)),
                  pl.BlockSpec((1, d), lambda i, k: (0, 0))],
        out_specs=pl.BlockSpec((tm, d), lambda i, k: (i, 0)),
        out_shape=jax.ShapeDtypeStruct((m, d), F32),
        scratch_shapes=[pltpu.VMEM((tm, d), BF16), pltpu.VMEM((tm, d), F32)],
        compiler_params=_cparams(("parallel", "arbitrary")),
        name="ffn",
    )(x, g.reshape(1, d), wg_bf, wu_bf, wd_bf, g_final.reshape(1, d))


def _permute_w_in(w_in):
    sizes = (D_SSM, D_ATT, KV_HEADS * HEAD_DIM, KV_HEADS * HEAD_DIM, IDX_HEADS * IDX_DIM, IDX_DIM,
             IDX_HEADS, D_MODEL, D_MODEL)
    offs = np.concatenate([[0], np.cumsum(sizes)])
    u, q, k, v, qi, ki, wi, ga, gb = [w_in[:, offs[i]:offs[i + 1]] for i in range(9)]
    pad = jnp.zeros((w_in.shape[0], D_IN_PAD - COL_KW - IDX_DIM - IDX_HEADS), w_in.dtype)
    return jnp.concatenate([u, q, ga, gb, qi, k, v, ki, wi, pad], axis=1).astype(BF16)


def kernel(x_prompt, x_sample, cache_k, cache_v, cache_kidx, state_ssm_re, state_ssm_im, page_table, g_mix, w_in, ssm_a_re, ssm_a_im, ssm_log_dt, ssm_b_re, ssm_b_im, ssm_c_re, ssm_c_im, ssm_d, ssm_w_glu, p_a, p_b, w_o, g_ffn, w_gate, w_up, w_down, g_final):
    n_b, n_s, d = x_prompt.shape
    n_db, n_new, _ = x_sample.shape
    depth = w_in.shape[0]
    n_g, n_p = ssm_a_re.shape[1:]
    gp = n_g * n_p
    past = page_table.shape[1] * PAGE_SIZE
    n_phys = cache_k.shape[1]
    kv_w = KV_HEADS * HEAD_DIM

    tq = 128
    ts = 512
    tab_p = rope_tables(jnp.arange(n_s))
    tab_s = jnp.tile(rope_tables(past + jnp.arange(n_new)), (1, n_db, 1))
    cache_k2 = cache_k.reshape(depth, n_phys, PAGE_SIZE, kv_w)
    cache_v2 = cache_v.reshape(depth, n_phys, PAGE_SIZE, kv_w)

    xp = x_prompt.reshape(n_b * n_s, d)
    xs = x_sample.reshape(n_db * n_new, d)
    outs = {name: [] for name in ('kp', 'vp', 'kip', 'hrp', 'hip', 'ks', 'vs', 'kis', 'hrs', 'his')}
    zeros_h = jnp.zeros((n_b, 2 * gp), F32)
    for l in range(depth):
        w_in_bf = _permute_w_in(w_in[l])
        sp = s5_params(ssm_a_re[l], ssm_a_im[l], ssm_log_dt[l], ssm_b_re[l], ssm_b_im[l],
                       ssm_c_re[l], ssm_c_im[l], ssm_d[l], ssm_w_glu[l])
        pa_bf, pb_bf, wo_bf = p_a[l].astype(BF16), p_b[l].astype(BF16), w_o[l].astype(BF16)
        wg_bf, wu_bf, wd_bf = w_gate[l].astype(BF16), w_up[l].astype(BF16), w_down[l].astype(BF16)
        last = l == depth - 1

        z = norm_proj(xp, g_mix[l], w_in_bf)
        qz, qi_r, k_r, kbf, v, ki_r, kcat, vt, wt = rope_prep(z, tab_p, ts, True)
        y_a, ht = s5_branch(z, zeros_h, sp, n_b, n_s, False)
        y_b = prompt_attention(qi_r, wt, qz, kcat, kbf, vt, n_b, n_s, tq, ts)
        x1 = merge_proj(xp, y_a, y_b, z, pa_bf, pb_bf, wo_bf)
        xp = ffn(x1, g_ffn[l], wg_bf, wu_bf, wd_bf, g_final, last)
        outs['kp'].append(k_r.reshape(n_b, n_s, KV_HEADS, HEAD_DIM))
        outs['vp'].append(v.reshape(n_b, n_s, KV_HEADS, HEAD_DIM))
        outs['kip'].append(ki_r.reshape(n_b, n_s, IDX_DIM))
        outs['hrp'].append(ht[:, :gp].reshape(n_b, n_g, n_p))
        outs['hip'].append(ht[:, gp:].reshape(n_b, n_g, n_p))

        z = norm_proj(xs, g_mix[l], w_in_bf)
        qz, qi_r, k_r, kbf, v, ki_r, kcat = rope_prep(z, tab_s, n_db * n_new, False)
        h0 = jnp.concatenate([state_ssm_re[l].reshape(n_db, gp), state_ssm_im[l].reshape(n_db, gp)], axis=1)
        y_a, ht = s5_branch(z, h0, sp, n_db, n_new, True)
        y_b = sample_attention(l, page_table, cache_kidx, cache_k2, cache_v2, qi_r, z, qz, ki_r, k_r, v,
                               n_db, n_new)
        x1 = merge_proj(xs, y_a, y_b, z, pa_bf, pb_bf, wo_bf)
        xs = ffn(x1, g_ffn[l], wg_bf, wu_bf, wd_bf, g_final, last)
        outs['ks'].append(k_r.reshape(n_db, n_new, KV_HEADS, HEAD_DIM))
        outs['vs'].append(v.reshape(n_db, n_new, KV_HEADS, HEAD_DIM))
        outs['kis'].append(ki_r.reshape(n_db, n_new, IDX_DIM))
        outs['hrs'].append(ht[:, :gp].reshape(n_db, n_g, n_p))
        outs['his'].append(ht[:, gp:].reshape(n_db, n_g, n_p))

    st = lambda name: jnp.stack(outs[name])
    return (xp.reshape(n_b, n_s, d), xs.reshape(n_db, n_new, d),
            st('kp'), st('vp'), st('kip'), st('hrp'), st('hip'),
            st('ks'), st('vs'), st('kis'), st('hrs'), st('his'))
```

```python
import functools
import math

import jax
import jax.numpy as jnp
import numpy as np
from jax import lax
from jax.experimental import pallas as pl
from jax.experimental.pallas import tpu as pltpu

F32 = jnp.float32
BF16 = jnp.bfloat16
I32 = jnp.int32

EPS = 1e-6
PAGE_SIZE = 128
SSM_GROUP_CH = 16
SSM_STATE = 64
N_HEADS = 8
HEAD_DIM = 64
KV_HEADS = 2
ROT_DIM = HEAD_DIM // 4
ROPE_THETA = 500000.0
IDX_HEADS = 8
IDX_DIM = 64
TOPK_MAX = 256

LANES = 128
SUBLANES = 8
VMEM_LIMIT = 56 * 1024 * 1024

INT_MIN = -(2 ** 31)
INT_MAX = 2 ** 31 - 1
NEG_MASKED = -2e30
NEG_INIT = -1e30
LOG2_E = math.log2(math.e)

D_MODEL = 1024
D_SSM = 512
D_ATT = 512
COL_U = 0
COL_Q = 512
COL_GA = 1024
COL_GB = 2048
COL_QI = 3072
COL_K = 3584
COL_V = 3712
COL_KW = 3840
D_IN_PAD = 3968

SAMPLE_PAGES_PER_CHUNK = 8


def _cparams(sem):
    return pltpu.CompilerParams(dimension_semantics=sem, vmem_limit_bytes=VMEM_LIMIT)


def _sigmoid(x):
    return 1.0 / (1.0 + jnp.exp(-x))


def _rms(x, g):
    return x * lax.rsqrt(jnp.mean(x * x, axis=-1, keepdims=True) + EPS) * g


def _norm_proj_kernel(x_ref, g_ref, w_ref, o_ref, *, col_chunk):
    h = _rms(x_ref[...], g_ref[...]).astype(BF16)
    n = o_ref.shape[1]
    for c0 in range(0, n, col_chunk):
        c1 = min(n, c0 + col_chunk)
        o_ref[:, c0:c1] = jnp.dot(h, w_ref[:, c0:c1], preferred_element_type=F32)


def norm_proj(x, g, w_bf):
    m, d = x.shape
    n = w_bf.shape[1]
    tm = min(256, m)
    return pl.pallas_call(
        functools.partial(_norm_proj_kernel, col_chunk=512),
        grid=(m // tm,),
        in_specs=[pl.BlockSpec((tm, d), lambda i: (i, 0)),
                  pl.BlockSpec((1, d), lambda i: (0, 0)),
                  pl.BlockSpec((d, n), lambda i: (0, 0))],
        out_specs=pl.BlockSpec((tm, n), lambda i: (i, 0)),
        out_shape=jax.ShapeDtypeStruct((m, n), F32),
        compiler_params=_cparams(("parallel",)),
        name="norm_proj",
    )(x, g.reshape(1, d), w_bf)


def _rope_kernel(q_ref, qi_ref, k_ref, v_ref, kw_ref, tab_ref,
                 qz_ref, qir_ref, kr_ref, kbf_ref, vo_ref, kir_ref, kcat_ref, *t_refs, transposed):
    c = tab_ref[0]
    s1 = tab_ref[1]
    s2 = tab_ref[2]

    def rope(x):
        return x * c + pltpu.roll(x, LANES - ROT_DIM // 2, 1) * s1 + pltpu.roll(x, ROT_DIM // 2, 1) * s2

    lane = lax.broadcasted_iota(I32, c.shape, 1)
    lo_half = lane < HEAD_DIM
    heads_per_blk = LANES // HEAD_DIM
    q_rep = N_HEADS // KV_HEADS
    for blk in range(D_ATT // LANES):
        sl = slice(blk * LANES, (blk + 1) * LANES)
        qir_ref[:, sl] = rope(qi_ref[:, sl])
        x = rope(q_ref[:, sl]) * (HEAD_DIM ** -0.5 * LOG2_E)
        xr = pltpu.roll(x, HEAD_DIM, 1)
        for par in range(heads_per_blk):
            h = blk * heads_per_blk + par
            g = h // q_rep
            src = x if par == g else xr
            keep = lo_half if g == 0 else jnp.logical_not(lo_half)
            qz_ref[:, h * LANES:(h + 1) * LANES] = jnp.where(keep, src, 0.0)
    kr = rope(k_ref[...])
    kr_ref[...] = kr
    kbf_ref[...] = kr.astype(BF16)
    v = v_ref[...]
    vo_ref[...] = v
    kw = kw_ref[...]
    kwr = rope(kw)
    kir_ref[...] = kwr[:, :IDX_DIM]
    kd = jnp.where(lo_half, kwr, pltpu.roll(kwr, IDX_DIM, 1))
    hi = kd.astype(BF16)
    lo = kd - hi.astype(F32)
    kcat_ref[:, :LANES] = hi
    kcat_ref[:, LANES:] = jnp.where(lo_half, lo, 0.0).astype(BF16)
    if transposed:
        vt_ref, wt_ref = t_refs
        vt_ref[0] = v.T.astype(BF16)
        wt_ref[...] = kw.T[IDX_DIM:IDX_DIM + IDX_HEADS, :]


def rope_prep(z, tab, tt, transposed):
    m = z.shape[0]
    n_t = tab.shape[1] // tt
    row = lambda i: i
    in_specs = [pl.BlockSpec((tt, D_ATT), lambda i: (row(i), COL_Q // D_ATT)),
                pl.BlockSpec((tt, D_ATT), lambda i: (row(i), COL_QI // D_ATT)),
                pl.BlockSpec((tt, LANES), lambda i: (row(i), COL_K // LANES)),
                pl.BlockSpec((tt, LANES), lambda i: (row(i), COL_V // LANES)),
                pl.BlockSpec((tt, LANES), lambda i: (row(i), COL_KW // LANES)),
                pl.BlockSpec((3, tt, LANES), lambda i: (0, i % n_t, 0))]
    out_specs = [pl.BlockSpec((tt, N_HEADS * LANES), lambda i: (i, 0)),
                 pl.BlockSpec((tt, D_ATT), lambda i: (i, 0)),
                 pl.BlockSpec((tt, LANES), lambda i: (i, 0)),
                 pl.BlockSpec((tt, LANES), lambda i: (i, 0)),
                 pl.BlockSpec((tt, LANES), lambda i: (i, 0)),
                 pl.BlockSpec((tt, IDX_DIM), lambda i: (i, 0)),
                 pl.BlockSpec((tt, 2 * LANES), lambda i: (i, 0))]
    out_shape = [jax.ShapeDtypeStruct((m, N_HEADS * LANES), F32),
                 jax.ShapeDtypeStruct((m, D_ATT), F32),
                 jax.ShapeDtypeStruct((m, LANES), F32),
                 jax.ShapeDtypeStruct((m, LANES), BF16),
                 jax.ShapeDtypeStruct((m, LANES), F32),
                 jax.ShapeDtypeStruct((m, IDX_DIM), F32),
                 jax.ShapeDtypeStruct((m, 2 * LANES), BF16)]
    if transposed:
        out_specs += [pl.BlockSpec((1, LANES, tt), lambda i: (i, 0, 0)),
                      pl.BlockSpec((IDX_HEADS, tt), lambda i: (0, i))]
        out_shape += [jax.ShapeDtypeStruct((m // tt, LANES, tt), BF16),
                      jax.ShapeDtypeStruct((IDX_HEADS, m), F32)]
    return pl.pallas_call(
        functools.partial(_rope_kernel, transposed=transposed),
        grid=(m // tt,),
        in_specs=in_specs, out_specs=out_specs, out_shape=out_shape,
        compiler_params=_cparams(("parallel",)),
        name="rope_prep",
    )(z, z, z, z, z, tab)


def rope_tables(pos):
    half = ROT_DIM // 2
    freqs = ROPE_THETA ** (-jnp.arange(half, dtype=F32) / half)
    ang = pos.astype(F32)[:, None] * freqs[None, :]
    cos, sin = jnp.cos(ang), jnp.sin(ang)
    n = pos.shape[0]
    one = jnp.ones((n, HEAD_DIM - ROT_DIM), F32)
    zero = jnp.zeros((n, HEAD_DIM - ROT_DIM), F32)
    zh = jnp.zeros((n, half), F32)
    c = jnp.concatenate([cos, cos, one], axis=1)
    s1 = jnp.concatenate([-sin, zh, zero], axis=1)
    s2 = jnp.concatenate([zh, sin, zero], axis=1)
    tab = jnp.stack([c, s1, s2])
    return jnp.tile(tab, (1, 1, LANES // HEAD_DIM))


def _gelu_tanh(x):
    return 0.5 * x * (1.0 + jnp.tanh(math.sqrt(2.0 / math.pi) * (x + 0.044715 * (x * x * x))))


def _s5_kernel(u_ref, h0_ref, wb_ref, cst_ref, wc_ref, d_ref, wglu_ref, y_ref, ht_ref,
               bb_scr, hs_scr, carry_scr, *, independent_slabs):
    tc = u_ref.shape[0]
    n_blk, ub, sb = wb_ref.shape
    half = sb // 2
    gp = n_blk * half
    n_slab = tc // SUBLANES
    for j in range(n_blk):
        bb_scr[:, j * sb:(j + 1) * sb] = jnp.dot(u_ref[:, j * ub:(j + 1) * ub].astype(BF16), wb_ref[j],
                                                 preferred_element_type=F32)

    if not independent_slabs:
        @pl.when(pl.program_id(1) == 0)
        def _():
            carry_scr[...] = h0_ref[0]

    for col in range(gp // LANES):
        j, off = divmod(col * LANES, half)
        cre = slice(j * sb + off, j * sb + off + LANES)
        cim = slice(j * sb + half + off, j * sb + half + off + LANES)
        nre = slice(col * LANES, (col + 1) * LANES)
        nim = slice(gp + col * LANES, gp + (col + 1) * LANES)
        steps = [(1, cst_ref[0, :, nre], cst_ref[1, :, nre]),
                 (2, cst_ref[2, :, nre], cst_ref[3, :, nre]),
                 (4, cst_ref[4, :, nre], cst_ref[5, :, nre])]
        p_re = cst_ref[6, :, nre]
        p_im = cst_ref[7, :, nre]
        q_re = p_re[SUBLANES - 1:SUBLANES, :]
        q_im = p_im[SUBLANES - 1:SUBLANES, :]

        def body(r, carry):
            rows = pl.ds(pl.multiple_of(r * SUBLANES, SUBLANES), SUBLANES)
            hr = bb_scr[rows, cre]
            hi = bb_scr[rows, cim]
            for d, a_re, a_im in steps:
                sr = pltpu.roll(hr, d, 0)
                si = pltpu.roll(hi, d, 0)
                hr, hi = hr + a_re * sr - a_im * si, hi + a_re * si + a_im * sr
            if independent_slabs:
                c_re = h0_ref[r, :, nre]
                c_im = h0_ref[r, :, nim]
            else:
                c_re, c_im = carry
            last_re = hr[SUBLANES - 1:SUBLANES, :] + q_re * c_re - q_im * c_im
            last_im = hi[SUBLANES - 1:SUBLANES, :] + q_re * c_im + q_im * c_re
            hs_scr[rows, cre] = hr + p_re * c_re - p_im * c_im
            hs_scr[rows, cim] = hi + p_re * c_im + p_im * c_re
            if independent_slabs:
                ht_ref[r, :, nre] = last_re
                ht_ref[r, :, nim] = last_im
                return carry
            return last_re, last_im

        if independent_slabs:
            lax.fori_loop(0, n_slab, body, 0, unroll=4)
        else:
            out_re, out_im = lax.fori_loop(0, n_slab, body, (carry_scr[:, nre], carry_scr[:, nim]),
                                           unroll=4)
            carry_scr[:, nre] = out_re
            carry_scr[:, nim] = out_im

    if not independent_slabs:
        ht_ref[0] = carry_scr[...]
    g_parts = []
    for j in range(n_blk):
        cols = slice(j * ub, (j + 1) * ub)
        y = jnp.dot(hs_scr[:, j * sb:(j + 1) * sb].astype(BF16), wc_ref[j], preferred_element_type=F32)
        g_parts.append(_gelu_tanh(y + d_ref[:, cols] * u_ref[:, cols]))
    g = jnp.concatenate(g_parts, axis=1)
    gate = jnp.dot(g.astype(BF16), wglu_ref[...], preferred_element_type=F32)
    y_ref[...] = (g * _sigmoid(gate)).astype(BF16)


def s5_branch(z, h0, sp, n_b, n_t, independent_slabs):
    m = z.shape[0]
    n_blk, ub, sb = sp['wb'].shape
    gp2 = n_blk * sb
    const = lambda *shape: pl.BlockSpec(shape, lambda *_: (0,) * len(shape))
    w_specs = [const(n_blk, ub, sb), const(8, SUBLANES, gp2 // 2), const(n_blk, sb, ub),
               const(1, D_SSM), const(D_SSM, D_SSM)]
    if independent_slabs:
        assert n_t == SUBLANES
        tc = m
        grid = (1,)
        in_specs = [pl.BlockSpec((tc, D_SSM), lambda i: (0, COL_U // D_SSM)),
                    pl.BlockSpec((n_b, 1, gp2), lambda i: (0, 0, 0))] + w_specs
        out_specs = [pl.BlockSpec((tc, D_SSM), lambda i: (0, 0)),
                     pl.BlockSpec((n_b, 1, gp2), lambda i: (0, 0, 0))]
        ht_shape = (n_b, 1, gp2)
        h0_in = h0.reshape(n_b, 1, gp2)
        sem = ("arbitrary",)
    else:
        tc = 256
        n_c = n_t // tc
        grid = (n_b, n_c)
        in_specs = [pl.BlockSpec((tc, D_SSM), lambda b, j: (b * n_c + j, COL_U // D_SSM)),
                    pl.BlockSpec((1, 1, gp2), lambda b, j: (b, 0, 0))] + w_specs
        out_specs = [pl.BlockSpec((tc, D_SSM), lambda b, j: (b * n_c + j, 0)),
                     pl.BlockSpec((1, 1, gp2), lambda b, j: (b, 0, 0))]
        ht_shape = (n_b, 1, gp2)
        h0_in = h0.reshape(n_b, 1, gp2)
        sem = ("parallel", "arbitrary")
    y, ht = pl.pallas_call(
        functools.partial(_s5_kernel, independent_slabs=independent_slabs),
        grid=grid, in_specs=in_specs, out_specs=out_specs,
        out_shape=[jax.ShapeDtypeStruct((m, D_SSM), BF16), jax.ShapeDtypeStruct(ht_shape, F32)],
        scratch_shapes=[pltpu.VMEM((tc, gp2), F32), pltpu.VMEM((tc, gp2), F32), pltpu.VMEM((1, gp2), F32)],
        compiler_params=_cparams(sem),
        name="s5_branch",
    )(z, h0_in, sp['wb'], sp['cst'], sp['wc'], sp['d'], sp['wglu'])
    return y, ht.reshape(n_b, gp2)


def s5_params(a_re, a_im, log_dt, b_re, b_im, c_re, c_im, d, w_glu):
    n_g, n_p = a_re.shape
    dt = jnp.exp(log_dt)[:, None]
    mag = jnp.exp(dt * a_re)
    abar_re = mag * jnp.cos(dt * a_im)
    abar_im = mag * jnp.sin(dt * a_im)
    nr, ni = abar_re - 1.0, abar_im
    den = a_re * a_re + a_im * a_im
    coef_re = (nr * a_re + ni * a_im) / den
    coef_im = (ni * a_re - nr * a_im) / den
    bbar_re = coef_re[:, :, None] * b_re - coef_im[:, :, None] * b_im
    bbar_im = coef_re[:, :, None] * b_im + coef_im[:, :, None] * b_re
    gp = n_g * n_p
    d_ssm = n_g * SSM_GROUP_CH
    n_c = SSM_GROUP_CH
    gb = LANES // n_c
    n_blk = n_g // gb
    eye = jnp.eye(gb, dtype=F32)
    blk_b = lambda t: jnp.einsum('jgpc,gh->jgchp', t.reshape(n_blk, gb, n_p, n_c), eye
                                 ).reshape(n_blk, gb * n_c, gb * n_p)
    blk_c = lambda t: jnp.einsum('jgcp,gh->jgphc', t.reshape(n_blk, gb, n_c, n_p), eye
                                 ).reshape(n_blk, gb * n_p, gb * n_c)
    wb = jnp.concatenate([blk_b(bbar_re), blk_b(bbar_im)], axis=2).astype(BF16)
    wc = jnp.concatenate([blk_c(c_re), -blk_c(c_im)], axis=1).astype(BF16)
    ar, ai = abar_re.reshape(gp), abar_im.reshape(gp)
    pows = [(ar, ai)]
    for _ in range(SUBLANES - 1):
        pr, pi = pows[-1]
        pows.append((pr * ar - pi * ai, pr * ai + pi * ar))
    rows = jnp.arange(SUBLANES)[:, None]
    cst = []
    for dd in (1, 2, 4):
        pr, pi = pows[dd - 1]
        cst.append(jnp.where(rows >= dd, pr[None, :], 0.0))
        cst.append(jnp.where(rows >= dd, pi[None, :], 0.0))
    cst.append(jnp.stack([p[0] for p in pows]))
    cst.append(jnp.stack([p[1] for p in pows]))
    return {'wb': wb, 'wc': wc, 'cst': jnp.stack(cst), 'd': d.reshape(1, d_ssm),
            'wglu': w_glu.astype(BF16)}


def _tree_reduce(parts, op):
    while len(parts) > 1:
        parts = [op(parts[i], parts[i + 1]) if i + 1 < len(parts) else parts[i]
                 for i in range(0, len(parts), 2)]
    return parts[0]


def _tree_sum(parts):
    return _tree_reduce(parts, jnp.add)


def _tree_max(parts):
    return _tree_reduce(parts, jnp.maximum)


def _score_key(score):
    bits = pltpu.bitcast(score, I32)
    return bits ^ ((bits >> 31) & INT_MAX)


def _bisect_threshold(count_ge, topk, like):
    cnt0 = count_ge(jnp.zeros_like(like))
    prefix0 = jnp.where(cnt0 >= topk, 0, INT_MIN).astype(I32)

    def body(b, prefix):
        cand = prefix | lax.shift_left(jnp.int32(1), 30 - b)
        return jnp.where(count_ge(cand) >= topk, cand, prefix)

    return lax.fori_loop(0, 31, body, prefix0)


def _bisect_tie_index(count_eq_below, need, n_bits, like):
    def body(b, prefix):
        cand = prefix | lax.shift_left(jnp.int32(1), n_bits - 1 - b)
        return jnp.where(count_eq_below(cand) < need, cand, prefix)

    return lax.fori_loop(0, n_bits, body, jnp.zeros_like(like))


def _prompt_attn_kernel(qi_ref, wt_ref, qz_ref, kcat_ref, kbf_ref, vt_ref, o_ref,
                        qcat_scr, qzs_scr, keys_scr, tau_scr, jst_scr, m_scr, l_scr, acc_scr,
                        lt_scr, bias_scr, p_scr, alpha_scr,
                        *, topk, idx_bits):
    tq = qi_ref.shape[0]
    ts = vt_ref.shape[3]
    j = pl.program_id(1)
    n_tiles = (j * tq + tq + ts - 1) // ts

    lane = lax.broadcasted_iota(I32, (tq, LANES), 1)
    lo_half = lane < IDX_DIM
    for blk in range(IDX_HEADS * IDX_DIM // LANES):
        x = qi_ref[:, blk * LANES:(blk + 1) * LANES]
        xr = pltpu.roll(x, IDX_DIM, 1)
        for par in range(2):
            h = 2 * blk + par
            dup = jnp.where(lo_half, x, xr) if par == 0 else jnp.where(lo_half, xr, x)
            hi = dup.astype(BF16).astype(F32)
            lh = jnp.where(lo_half, hi, dup - hi).astype(BF16)
            qcat_scr[h * tq:(h + 1) * tq, :LANES] = lh
            qcat_scr[h * tq:(h + 1) * tq, LANES:] = lh
    for h in range(N_HEADS):
        qzs_scr[h * tq:(h + 1) * tq, :] = qz_ref[:, h * LANES:(h + 1) * LANES].astype(BF16)

    wt = wt_ref[...]
    tcol = j * tq + lax.broadcasted_iota(I32, (ts, tq), 1)
    srow0 = lax.broadcasted_iota(I32, (ts, tq), 0)
    nt_dims = (((1,), (1,)), ((), ()))

    def idx_body(i, _):
        s0 = pl.multiple_of(i * ts, ts)
        dt = lax.dot_general(kcat_ref[0, pl.ds(s0, ts), :], qcat_scr[...], nt_dims,
                             preferred_element_type=F32)
        sc = jnp.zeros((ts, tq), F32)
        for h in range(IDX_HEADS):
            sc = sc + wt[h:h + 1, :] * jnp.maximum(dt[:, h * tq:(h + 1) * tq], 0.0)
        key = jnp.where(srow0 + s0 <= tcol, _score_key(sc), INT_MIN)
        keys_scr[pl.ds(s0, ts), :] = key
        return 0

    lax.fori_loop(0, n_tiles, idx_body, 0)

    def count(pred):
        def body(i, acc):
            s0 = pl.multiple_of(i * ts, ts)
            hit = jnp.where(pred(keys_scr[pl.ds(s0, ts), :], s0), 1.0, 0.0)
            return acc + _tree_sum([hit[r:r + SUBLANES] for r in range(0, ts, SUBLANES)])
        acc = lax.fori_loop(0, n_tiles, body, jnp.zeros((SUBLANES, tq), F32))
        return jnp.sum(acc, axis=0, keepdims=True)

    like = jnp.zeros((1, tq), I32)
    kf = float(topk)
    tau = _bisect_threshold(lambda cand: count(lambda k, s0: k >= cand), kf, like)
    tau = jnp.maximum(tau, INT_MIN + 1)
    cnt_gt = count(lambda k, s0: k > tau)
    cnt_ge = count(lambda k, s0: k >= tau)
    need = kf - cnt_gt
    tie = (cnt_ge - cnt_gt) > need
    tau_scr[...] = tau
    jst_scr[...] = jnp.full((1, tq), INT_MAX, I32)

    @pl.when(jnp.max(jnp.where(tie, 1.0, 0.0)) > 0.0)
    def _():
        jst = _bisect_tie_index(
            lambda cand: count(lambda k, s0: (k == tau) & (srow0 + s0 < cand)), need, idx_bits, like)
        jst_scr[...] = jnp.where(tie, jst, INT_MAX)

    jst = jst_scr[...]
    m_scr[...] = jnp.full(m_scr.shape, NEG_INIT, F32)
    l_scr[...] = jnp.zeros(l_scr.shape, F32)
    acc_scr[...] = jnp.zeros(acc_scr.shape, F32)

    rc = 8 * SUBLANES
    heads = range(N_HEADS)

    def att_body(t, _):
        s0 = pl.multiple_of(t * ts, ts)
        lt_scr[...] = lax.dot_general(kbf_ref[0, pl.ds(s0, ts), :], qzs_scr[...], nt_dims,
                                      preferred_element_type=F32)
        key = keys_scr[pl.ds(s0, ts), :]
        sel = (key > tau) | ((key == tau) & (srow0 + s0 <= jst))
        bias_scr[...] = jnp.where(sel, 0.0, NEG_MASKED)

        def chunk(r0, h):
            return lt_scr[r0:r0 + rc, h * tq:(h + 1) * tq] + bias_scr[r0:r0 + rc, :]

        def fold(x, op):
            return _tree_reduce([x[r:r + SUBLANES] for r in range(0, rc, SUBLANES)], op)

        mx = [None] * N_HEADS
        for r0 in range(0, ts, rc):
            for h in heads:
                part = fold(chunk(r0, h), jnp.maximum)
                mx[h] = part if mx[h] is None else jnp.maximum(mx[h], part)
        m_new = []
        for h in heads:
            cols = slice(h * tq, (h + 1) * tq)
            m_old = m_scr[:, cols]
            m_h = jnp.maximum(m_old, jnp.max(mx[h], axis=0, keepdims=True))
            alpha_scr[:, cols] = jnp.exp2(m_old - m_h)
            m_scr[:, cols] = m_h
            m_new.append(jnp.broadcast_to(m_h, (SUBLANES, tq)))
        sm = [None] * N_HEADS
        for r0 in range(0, ts, rc):
            for h in heads:
                x = chunk(r0, h)
                p = jnp.exp2(x - jnp.concatenate([m_new[h]] * (rc // SUBLANES), axis=0))
                p_scr[r0:r0 + rc, h * tq:(h + 1) * tq] = p.astype(BF16)
                part = fold(p, jnp.add)
                sm[h] = part if sm[h] is None else sm[h] + part
        alpha = alpha_scr[...]
        l_scr[...] = alpha * l_scr[...] + jnp.concatenate(
            [jnp.sum(sm[h], axis=0, keepdims=True) for h in heads], axis=1)
        acc_scr[...] = alpha * acc_scr[...] + jnp.dot(vt_ref[0, t], p_scr[...],
                                                      preferred_element_type=F32)
        return 0

    lax.fori_loop(0, n_tiles, att_body, 0)

    accn = acc_scr[...] / l_scr[...]
    q_rep = N_HEADS // KV_HEADS
    for blk in range(D_ATT // LANES):
        h0 = 2 * blk
        g = h0 // q_rep
        rows = slice(g * HEAD_DIM, (g + 1) * HEAD_DIM)
        pair = jnp.concatenate([accn[rows, h0 * tq:(h0 + 1) * tq],
                                accn[rows, (h0 + 1) * tq:(h0 + 2) * tq]], axis=0)
        o_ref[:, blk * LANES:(blk + 1) * LANES] = pair.T.astype(BF16)


def prompt_attention(qi_r, wt, qz, kcat, kbf, vt, n_b, n_t, tq, ts):
    m = n_b * n_t
    n_q = n_t // tq
    topk = min(TOPK_MAX, n_t // 4)
    idx_bits = max(1, int(math.ceil(math.log2(n_t))))
    return pl.pallas_call(
        functools.partial(_prompt_attn_kernel, topk=topk, idx_bits=idx_bits),
        grid=(n_b, n_q),
        in_specs=[pl.BlockSpec((tq, D_ATT), lambda b, j: (b * n_q + j, 0)),
                  pl.BlockSpec((IDX_HEADS, tq), lambda b, j: (0, b * n_q + j)),
                  pl.BlockSpec((tq, N_HEADS * LANES), lambda b, j: (b * n_q + j, 0)),
                  pl.BlockSpec((1, n_t, 2 * LANES), lambda b, j: (b, 0, 0)),
                  pl.BlockSpec((1, n_t, LANES), lambda b, j: (b, 0, 0)),
                  pl.BlockSpec((1, n_t // ts, LANES, ts), lambda b, j: (b, 0, 0, 0))],
        out_specs=pl.BlockSpec((tq, D_ATT), lambda b, j: (b * n_q + j, 0)),
        out_shape=jax.ShapeDtypeStruct((m, D_ATT), BF16),
        scratch_shapes=[pltpu.VMEM((IDX_HEADS * tq, 2 * LANES), BF16),
                        pltpu.VMEM((N_HEADS * tq, LANES), BF16),
                        pltpu.VMEM((n_t, tq), I32),
                        pltpu.VMEM((1, tq), I32),
                        pltpu.VMEM((1, tq), I32),
                        pltpu.VMEM((1, N_HEADS * tq), F32),
                        pltpu.VMEM((1, N_HEADS * tq), F32),
                        pltpu.VMEM((LANES, N_HEADS * tq), F32),
                        pltpu.VMEM((ts, N_HEADS * tq), F32),
                        pltpu.VMEM((ts, tq), F32),
                        pltpu.VMEM((ts, N_HEADS * tq), BF16),
                        pltpu.VMEM((1, N_HEADS * tq), F32)],
        compiler_params=_cparams(("parallel", "arbitrary")),
        name="prompt_attention",
    )(qi_r, wt, qz, kcat.reshape(n_b, n_t, 2 * LANES), kbf.reshape(n_b, n_t, LANES),
      vt.reshape(n_b, n_t // ts, LANES, ts))


def _sample_attn_kernel(pt_ref, qi_ref, kw_ref, qz_ref, kin_ref, kn_ref, vn_ref,
                        kidx_hbm, k_hbm, v_hbm, o_ref,
                        kidx_buf, k_buf, v_buf, sems, qhl_scr, qzs_scr, keys_scr, lt_scr,
                        *, layer, pages_per_chunk, topk, idx_bits):
    b = pl.program_id(0)
    n_b = pl.num_programs(0)
    n_pages = k_buf.shape[0]
    pc = pages_per_chunk
    n_chunks = n_pages // pc
    ck = pc * PAGE_SIZE
    n_q = qi_ref.shape[1]
    n_rows = IDX_HEADS * n_q
    slot = b % 2

    def kidx_copy(seq, page, sl):
        return pltpu.make_async_copy(kidx_hbm.at[layer, pt_ref[seq, page]], kidx_buf.at[sl, page],
                                     sems.at[0, sl])

    def k_copy(page):
        return pltpu.make_async_copy(k_hbm.at[layer, pt_ref[b, page]], k_buf.at[page], sems.at[1, 0])

    def v_copy(page):
        return pltpu.make_async_copy(v_hbm.at[layer, pt_ref[b, page]], v_buf.at[page], sems.at[2, 0])

    def for_pages(fn):
        def body(page, carry):
            fn(page)
            return carry
        lax.fori_loop(0, n_pages, body, 0, unroll=8)

    @pl.when(b == 0)
    def _():
        for_pages(lambda page: kidx_copy(0, page, 0).start())

    for_pages(lambda page: k_copy(page).start())
    for_pages(lambda page: v_copy(page).start())

    @pl.when(b + 1 < n_b)
    def _():
        for_pages(lambda page: kidx_copy(b + 1, page, 1 - slot).start())

    qi = qi_ref[0]
    q_all = jnp.concatenate([qi[:, h * IDX_DIM:(h + 1) * IDX_DIM] for h in range(IDX_HEADS)], axis=0)
    q_hi = q_all.astype(BF16)
    qhl_scr[:n_rows, :] = q_hi
    qhl_scr[n_rows:, :] = (q_all - q_hi.astype(F32)).astype(BF16)
    qz = qz_ref[0]
    for h in range(N_HEADS):
        qzs_scr[h * n_q:(h + 1) * n_q, :] = qz[:, h * LANES:(h + 1) * LANES]
    w_all = kw_ref[0]
    lane_c = lax.broadcasted_iota(I32, (n_q, ck), 1)
    nt_dims = (((1,), (1,)), ((), ()))

    def scores(keys_t):
        khi = keys_t.astype(BF16)
        klo = (keys_t - khi.astype(F32)).astype(BF16)
        r1 = jnp.dot(qhl_scr[...], khi, preferred_element_type=F32)
        r2 = jnp.dot(qhl_scr[:n_rows, :], klo, preferred_element_type=F32)
        dots = r1[:n_rows] + r1[n_rows:] + r2
        sc = jnp.zeros((n_q, keys_t.shape[1]), F32)
        for h in range(IDX_HEADS):
            sc = sc + w_all[:, IDX_DIM + h:IDX_DIM + h + 1] * jnp.maximum(dots[h * n_q:(h + 1) * n_q], 0.0)
        return sc

    for_pages(lambda page: kidx_copy(b, page, slot).wait())

    def idx_body(c, carry):
        kc = jnp.concatenate([kidx_buf[slot, c * pc + i] for i in range(pc)], axis=1)
        keys_scr[c] = _score_key(scores(kc))
        return carry

    lax.fori_loop(0, n_chunks, idx_body, 0)
    qrow = lax.broadcasted_iota(I32, (n_q, LANES), 0)
    key_new = jnp.where(lane_c[:, :LANES] <= qrow, _score_key(scores(kin_ref[0])), INT_MIN)
    keys_scr[n_chunks] = jnp.concatenate([key_new, jnp.full((n_q, ck - LANES), INT_MIN, I32)], axis=1)

    def count(pred):
        def body(i, acc):
            hit = jnp.where(pred(keys_scr[i], i * ck), 1.0, 0.0)
            return acc + _tree_sum([hit[:, c * LANES:(c + 1) * LANES] for c in range(ck // LANES)])
        acc = lax.fori_loop(0, n_chunks + 1, body, jnp.zeros((n_q, LANES), F32), unroll=True)
        return jnp.sum(acc, axis=1, keepdims=True)

    like = jnp.zeros((n_q, 1), I32)
    kf = float(topk)
    tau = _bisect_threshold(lambda cand: count(lambda k, s0: k >= cand), kf, like)
    tau = jnp.maximum(tau, INT_MIN + 1)
    cnt_gt = count(lambda k, s0: k > tau)
    cnt_ge = count(lambda k, s0: k >= tau)
    need = kf - cnt_gt
    tie = (cnt_ge - cnt_gt) > need
    jst = lax.cond(
        jnp.max(jnp.where(tie, 1.0, 0.0)) > 0.0,
        lambda: jnp.where(tie, _bisect_tie_index(
            lambda cand: count(lambda k, s0: (k == tau) & (lane_c + s0 < cand)), need, idx_bits, like), INT_MAX),
        lambda: jnp.full((n_q, 1), INT_MAX, I32))

    def masked(lt, key, s0):
        n = lt.shape[1]
        sel = (key > tau) | ((key == tau) & (lane_c[:, :n] + s0 <= jst))
        return jnp.concatenate(
            [jnp.where(sel, lt[h * n_q:(h + 1) * n_q], NEG_MASKED) for h in range(N_HEADS)], axis=0)

    def lane_fold(x, op):
        return _tree_reduce([x[:, c * LANES:(c + 1) * LANES] for c in range(x.shape[1] // LANES)], op)

    for_pages(lambda page: k_copy(page).wait())
    for_pages(lambda page: v_copy(page).wait())
    qz_bf = qzs_scr[...].astype(BF16)

    def logits_body(c, mx):
        kc = jnp.concatenate([k_buf[c * pc + i] for i in range(pc)], axis=1).astype(BF16)
        lt = masked(jnp.dot(qz_bf, kc, preferred_element_type=F32), keys_scr[c], c * ck)
        lt_scr[c] = lt
        return jnp.maximum(mx, lane_fold(lt, jnp.maximum))

    mx = lax.fori_loop(0, n_chunks, logits_body, jnp.full((n_rows, LANES), NEG_INIT, F32))
    lt_new = masked(jnp.dot(qz_bf, kn_ref[0].astype(BF16), preferred_element_type=F32),
                    keys_scr[n_chunks][:, :LANES], n_chunks * ck)
    m = jnp.max(jnp.maximum(mx, lt_new), axis=1, keepdims=True)
    p_new = jnp.exp2(lt_new - m)
    acc0 = lax.dot_general(p_new.astype(BF16), vn_ref[0].astype(BF16), nt_dims, preferred_element_type=F32)

    def pv_body(c, carry):
        l_acc, acc = carry
        p = jnp.exp2(lt_scr[c] - m)
        vc = jnp.concatenate([v_buf[c * pc + i] for i in range(pc)], axis=1).astype(BF16)
        acc = acc + lax.dot_general(p.astype(BF16), vc, nt_dims, preferred_element_type=F32)
        return l_acc + lane_fold(p, jnp.add), acc

    l_acc, acc = lax.fori_loop(0, n_chunks, pv_body, (p_new, acc0))
    accn = acc / jnp.sum(l_acc, axis=1, keepdims=True)
    lo_half = lax.broadcasted_iota(I32, (n_q, LANES), 1) < HEAD_DIM
    q_rep = N_HEADS // KV_HEADS
    for blk in range(D_ATT // LANES):
        a = accn[(2 * blk) * n_q:(2 * blk + 1) * n_q]
        bb = accn[(2 * blk + 1) * n_q:(2 * blk + 2) * n_q]
        if (2 * blk) // q_rep == 0:
            out = jnp.where(lo_half, a, pltpu.roll(bb, HEAD_DIM, 1))
        else:
            out = jnp.where(lo_half, pltpu.roll(a, HEAD_DIM, 1), bb)
        o_ref[0, :, blk * LANES:(blk + 1) * LANES] = out.astype(BF16)


def sample_attention(layer, page_table, kidx_t, k_t, v_t, qi_r, z, qz, ki_r, k_r, v, n_b, n_q):
    n_pages = page_table.shape[1]
    pc = SAMPLE_PAGES_PER_CHUNK
    n_chunks = n_pages // pc
    past = n_pages * PAGE_SIZE
    topk = min(TOPK_MAX, (past + n_q) // 4)
    idx_bits = int(math.ceil(math.log2((n_chunks + 1) * pc * PAGE_SIZE)))
    kv_w = KV_HEADS * HEAD_DIM

    def new_keys_t(a):
        at = a.reshape(n_b, n_q, a.shape[-1]).transpose(0, 2, 1)
        return jnp.pad(at, ((0, 0), (0, 0), (0, PAGE_SIZE - n_q)))

    per_b = lambda rows, width: pl.BlockSpec((1, rows, width), lambda b, pt: (b, 0, 0))
    any_spec = pl.BlockSpec(memory_space=pl.ANY)
    n_rows = N_HEADS * n_q
    grid_spec = pltpu.PrefetchScalarGridSpec(
        num_scalar_prefetch=1, grid=(n_b,),
        in_specs=[per_b(n_q, D_ATT),
                  pl.BlockSpec((1, n_q, LANES), lambda b, pt: (b, 0, COL_KW // LANES)),
                  per_b(n_q, N_HEADS * LANES), per_b(IDX_DIM, PAGE_SIZE), per_b(kv_w, PAGE_SIZE),
                  per_b(kv_w, PAGE_SIZE), any_spec, any_spec, any_spec],
        out_specs=pl.BlockSpec((1, n_q, D_ATT), lambda b, pt: (b, 0, 0)),
        scratch_shapes=[pltpu.VMEM((2, n_pages, IDX_DIM, PAGE_SIZE), F32),
                        pltpu.VMEM((n_pages, kv_w, PAGE_SIZE), F32),
                        pltpu.VMEM((n_pages, kv_w, PAGE_SIZE), F32),
                        pltpu.SemaphoreType.DMA((3, 2)),
                        pltpu.VMEM((2 * n_rows, IDX_DIM), BF16),
                        pltpu.VMEM((n_rows, LANES), F32),
                        pltpu.VMEM((n_chunks + 1, n_q, pc * PAGE_SIZE), I32),
                        pltpu.VMEM((n_chunks, n_rows, pc * PAGE_SIZE), F32)])
    y = pl.pallas_call(
        functools.partial(_sample_attn_kernel, layer=layer, pages_per_chunk=pc, topk=topk, idx_bits=idx_bits),
        grid_spec=grid_spec,
        out_shape=jax.ShapeDtypeStruct((n_b, n_q, D_ATT), BF16),
        compiler_params=_cparams(("arbitrary",)),
        name="sample_attention",
    )(page_table, qi_r.reshape(n_b, n_q, D_ATT), z.reshape(n_b, n_q, z.shape[-1]),
      qz.reshape(n_b, n_q, N_HEADS * LANES), new_keys_t(ki_r), new_keys_t(k_r), new_keys_t(v),
      kidx_t, k_t, v_t)
    return y.reshape(n_b * n_q, D_ATT)


def _merge_kernel(x_ref, ya_ref, yb_ref, ga_ref, gb_ref, pa_ref, pb_ref, wo_ref, o_ref):
    a = jnp.dot(ya_ref[...], pa_ref[...], preferred_element_type=F32)
    b = jnp.dot(yb_ref[...], pb_ref[...], preferred_element_type=F32)
    mixed = _sigmoid(ga_ref[...]) * a + _sigmoid(gb_ref[...]) * b
    o_ref[...] = x_ref[...] + jnp.dot(mixed.astype(BF16), wo_ref[...], preferred_element_type=F32)


def merge_proj(x, y_a, y_b, z, pa_bf, pb_bf, wo_bf):
    m, d = x.shape
    tm = min(256, m)
    const = lambda *shape: pl.BlockSpec(shape, lambda i: (0,) * len(shape))
    return pl.pallas_call(
        _merge_kernel,
        grid=(m // tm,),
        in_specs=[pl.BlockSpec((tm, d), lambda i: (i, 0)),
                  pl.BlockSpec((tm, D_SSM), lambda i: (i, 0)),
                  pl.BlockSpec((tm, D_ATT), lambda i: (i, 0)),
                  pl.BlockSpec((tm, d), lambda i: (i, COL_GA // d)),
                  pl.BlockSpec((tm, d), lambda i: (i, COL_GB // d)),
                  const(D_SSM, d), const(D_ATT, d), const(d, d)],
        out_specs=pl.BlockSpec((tm, d), lambda i: (i, 0)),
        out_shape=jax.ShapeDtypeStruct((m, d), F32),
        compiler_params=_cparams(("parallel",)),
        name="merge_proj",
    )(x, y_a, y_b, z, z, pa_bf, pb_bf, wo_bf)


def _ffn_kernel(x_ref, g_ref, wg_ref, wu_ref, wd_ref, gf_ref, o_ref, h_scr, acc_scr, *, final_norm):
    k = pl.program_id(1)

    @pl.when(k == 0)
    def _():
        h_scr[...] = _rms(x_ref[...], g_ref[...]).astype(BF16)
        acc_scr[...] = jnp.zeros(acc_scr.shape, F32)

    h = h_scr[...]
    gate = jnp.dot(h, wg_ref[...], preferred_element_type=F32)
    up = jnp.dot(h, wu_ref[...], preferred_element_type=F32)
    act = (gate * _sigmoid(gate) * up).astype(BF16)
    acc_scr[...] += jnp.dot(act, wd_ref[...], preferred_element_type=F32)

    @pl.when(k == pl.num_programs(1) - 1)
    def _():
        y = x_ref[...] + acc_scr[...]
        o_ref[...] = _rms(y, gf_ref[...]) if final_norm else y


def ffn(x, g, wg_bf, wu_bf, wd_bf, g_final, final_norm):
    m, d = x.shape
    d_ff = wg_bf.shape[1]
    tm = 1024 if m % 1024 == 0 else min(256, m)
    tf = 256
    return pl.pallas_call(
        functools.partial(_ffn_kernel, final_norm=final_norm),
        grid=(m // tm, d_ff // tf),
        in_specs=[pl.BlockSpec((tm, d), lambda i, k: (i, 0)),
                  pl.BlockSpec((1, d), lambda i, k: (0, 0)),
                  pl.BlockSpec((d, tf), lambda i, k: (0, k)),
                  pl.BlockSpec((d, tf), lambda i, k: (0, k)),
                  pl.BlockSpec((tf, d), lambda i, k: (k, 0)),
                  pl.BlockSpec((1, d), lambda i, k: (0, 0))],
        out_specs=pl.BlockSpec((tm, d), lambda i, k: (i, 0)),
        out_shape=jax.ShapeDtypeStruct((m, d), F32),
        scratch_shapes=[pltpu.VMEM((tm, d), BF16), pltpu.VMEM((tm, d), F32)],
        compiler_params=_cparams(("parallel", "arbitrary")),
        name="ffn",
    )(x, g.reshape(1, d), wg_bf, wu_bf, wd_bf, g_final.reshape(1, d))


def _permute_w_in(w_in):
    sizes = (D_SSM, D_ATT, KV_HEADS * HEAD_DIM, KV_HEADS * HEAD_DIM, IDX_HEADS * IDX_DIM, IDX_DIM,
             IDX_HEADS, D_MODEL, D_MODEL)
    offs = np.concatenate([[0], np.cumsum(sizes)])
    u, q, k, v, qi, ki, wi, ga, gb = [w_in[:, offs[i]:offs[i + 1]] for i in range(9)]
    pad = jnp.zeros((w_in.shape[0], D_IN_PAD - COL_KW - IDX_DIM - IDX_HEADS), w_in.dtype)
    return jnp.concatenate([u, q, ga, gb, qi, k, v, ki, wi, pad], axis=1).astype(BF16)


def kernel(x_prompt, x_sample, cache_k, cache_v, cache_kidx, state_ssm_re, state_ssm_im, page_table, g_mix, w_in, ssm_a_re, ssm_a_im, ssm_log_dt, ssm_b_re, ssm_b_im, ssm_c_re, ssm_c_im, ssm_d, ssm_w_glu, p_a, p_b, w_o, g_ffn, w_gate, w_up, w_down, g_final):
    n_b, n_s, d = x_prompt.shape
    n_db, n_new, _ = x_sample.shape
    depth = w_in.shape[0]
    n_g, n_p = ssm_a_re.shape[1:]
    gp = n_g * n_p
    past = page_table.shape[1] * PAGE_SIZE
    n_phys = cache_k.shape[1]
    kv_w = KV_HEADS * HEAD_DIM

    tq = 128
    ts = 512
    tab_p = rope_tables(jnp.arange(n_s))
    tab_s = jnp.tile(rope_tables(past + jnp.arange(n_new)), (1, n_db, 1))
    kidx_t = cache_kidx.transpose(0, 1, 3, 2)
    k_t = cache_k.transpose(0, 1, 3, 4, 2).reshape(depth, n_phys, kv_w, PAGE_SIZE)
    v_t = cache_v.transpose(0, 1, 3, 4, 2).reshape(depth, n_phys, kv_w, PAGE_SIZE)

    xp = x_prompt.reshape(n_b * n_s, d)
    xs = x_sample.reshape(n_db * n_new, d)
    outs = {name: [] for name in ('kp', 'vp', 'kip', 'hrp', 'hip', 'ks', 'vs', 'kis', 'hrs', 'his')}
    zeros_h = jnp.zeros((n_b, 2 * gp), F32)
    for l in range(depth):
        w_in_bf = _permute_w_in(w_in[l])
        sp = s5_params(ssm_a_re[l], ssm_a_im[l], ssm_log_dt[l], ssm_b_re[l], ssm_b_im[l],
                       ssm_c_re[l], ssm_c_im[l], ssm_d[l], ssm_w_glu[l])
        pa_bf, pb_bf, wo_bf = p_a[l].astype(BF16), p_b[l].astype(BF16), w_o[l].astype(BF16)
        wg_bf, wu_bf, wd_bf = w_gate[l].astype(BF16), w_up[l].astype(BF16), w_down[l].astype(BF16)
        last = l == depth - 1

        z = norm_proj(xp, g_mix[l], w_in_bf)
        qz, qi_r, k_r, kbf, v, ki_r, kcat, vt, wt = rope_prep(z, tab_p, ts, True)
        y_a, ht = s5_branch(z, zeros_h, sp, n_b, n_s, False)
        y_b = prompt_attention(qi_r, wt, qz, kcat, kbf, vt, n_b, n_s, tq, ts)
        x1 = merge_proj(xp, y_a, y_b, z, pa_bf, pb_bf, wo_bf)
        xp = ffn(x1, g_ffn[l], wg_bf, wu_bf, wd_bf, g_final, last)
        outs['kp'].append(k_r.reshape(n_b, n_s, KV_HEADS, HEAD_DIM))
        outs['vp'].append(v.reshape(n_b, n_s, KV_HEADS, HEAD_DIM))
        outs['kip'].append(ki_r.reshape(n_b, n_s, IDX_DIM))
        outs['hrp'].append(ht[:, :gp].reshape(n_b, n_g, n_p))
        outs['hip'].append(ht[:, gp:].reshape(n_b, n_g, n_p))

        z = norm_proj(xs, g_mix[l], w_in_bf)
        qz, qi_r, k_r, kbf, v, ki_r, kcat = rope_prep(z, tab_s, n_db * n_new, False)
        h0 = jnp.concatenate([state_ssm_re[l].reshape(n_db, gp), state_ssm_im[l].reshape(n_db, gp)], axis=1)
        y_a, ht = s5_branch(z, h0, sp, n_db, n_new, True)
        y_b = sample_attention(l, page_table, kidx_t, k_t, v_t, qi_r, z, qz, ki_r, k_r, v, n_db, n_new)
        x1 = merge_proj(xs, y_a, y_b, z, pa_bf, pb_bf, wo_bf)
        xs = ffn(x1, g_ffn[l], wg_bf, wu_bf, wd_bf, g_final, last)
        outs['ks'].append(k_r.reshape(n_db, n_new, KV_HEADS, HEAD_DIM))
        outs['vs'].append(v.reshape(n_db, n_new, KV_HEADS, HEAD_DIM))
        outs['kis'].append(ki_r.reshape(n_db, n_new, IDX_DIM))
        outs['hrs'].append(ht[:, :gp].reshape(n_db, n_g, n_p))
        outs['his'].append(ht[:, gp:].reshape(n_db, n_g, n_p))

    st = lambda name: jnp.stack(outs[name])
    return (xp.reshape(n_b, n_s, d), xs.reshape(n_db, n_new, d),
            st('kp'), st('vp'), st('kip'), st('hrp'), st('hip'),
            st('ks'), st('vs'), st('kis'), st('hrs'), st('his'))
```
